```python
import math
import jax
import jax.numpy as jnp
from jax import lax
import numpy as np

D_MODEL = 2048
BATCH = 1
SEQ = 8192
DEPTH = 1
DEC_BATCH = 128
DEC_SEQ = 8
PAST_LEN = 2048
PAGE_SIZE = 128

HEAD_DIM = 128
D_ATTN = D_MODEL // 2
N_HEADS_A = D_ATTN // HEAD_DIM
D_DELTA = D_MODEL - D_ATTN
N_HEADS_B = D_DELTA // HEAD_DIM
DK = HEAD_DIM
DV = HEAD_DIM
CONV_W = 4
CONV_DIM = 2 * N_HEADS_B * DK + N_HEADS_B * DV
PROJ_SPLITS = (D_ATTN, D_ATTN, D_ATTN, CONV_DIM, N_HEADS_B * DV, N_HEADS_B, N_HEADS_B)
PROJ_DIM = sum(PROJ_SPLITS)
DILATED_PATTERNS = ((128, 1), (512, 4), (2048, 16))
MAX_WINDOW = 2048
ATTN_BLOCK = 128
GDN_CHUNK = 64
ROPE_THETA = 10000.0
N_EXPERTS = 32
TOP_K = 4
D_FF = D_MODEL
SWIGLU_LIMIT = 7.0
SWIGLU_ALPHA = 1.702
MOE_BLOCK = 128
NORM_EPS = 1e-6

kernel_name = 'hybrid_dilated_gdn_moe_step'


def rmsnorm(x, g):
    xf = x.astype(jnp.float32)
    y = xf * lax.rsqrt(jnp.mean(xf * xf, axis=-1, keepdims=True) + NORM_EPS)
    return (y * g.astype(jnp.float32)).astype(x.dtype)


def l2norm(x):
    return x * lax.rsqrt(jnp.sum(x * x, axis=-1, keepdims=True) + 1e-6)


def rope(x, pos):
    half = x.shape[-1] // 2
    inv_freq = ROPE_THETA ** (-jnp.arange(half, dtype=jnp.float32) / half)
    ang = pos.astype(jnp.float32)[:, None] * inv_freq[None, :]
    cos = jnp.cos(ang)[None, :, None, :]
    sin = jnp.sin(ang)[None, :, None, :]
    xf = x.astype(jnp.float32)
    x1, x2 = xf[..., :half], xf[..., half:]
    return jnp.concatenate([x1 * cos - x2 * sin, x2 * cos + x1 * sin], axis=-1).astype(x.dtype)


def combine_dilated(outs, lses):
    w = jax.nn.softmax(jnp.stack(lses, axis=0), axis=0)
    return jnp.sum(w[..., None] * jnp.stack(outs, axis=0), axis=0)


def dilated_attention_prompt(q, k, v):
    B, S, H, D = q.shape
    q = q.astype(jnp.float32) * (D ** -0.5)
    k = k.astype(jnp.float32)
    v = v.astype(jnp.float32)
    qi = jnp.arange(ATTN_BLOCK)[:, None]
    kj = jnp.arange(2 * ATTN_BLOCK)[None, :]
    dist = qi + ATTN_BLOCK - kj
    outs, lses = [], []
    for window, dil in DILATED_PATTERNS:
        n_tap = window // dil
        span = dil * ATTN_BLOCK
        s_pad = -(-S // span) * span
        nb = s_pad // span
        pad = ((0, 0), (0, s_pad - S), (0, 0), (0, 0))
        qb, kb, vb = (jnp.pad(t, pad).reshape(B, nb, ATTN_BLOCK, dil, H, D) for t in (q, k, v))
        prev = ((0, 0), (1, 0), (0, 0), (0, 0), (0, 0), (0, 0))
        kk = jnp.concatenate([jnp.pad(kb, prev)[:, :-1], kb], axis=2)
        vv = jnp.concatenate([jnp.pad(vb, prev)[:, :-1], vb], axis=2)
        s = jnp.einsum('bnirhd,bnjrhd->bnrhij', qb, kk)
        first = (jnp.arange(nb) == 0)[:, None, None]
        valid = (dist >= 0) & (dist <= n_tap) & ~(first & (kj < ATTN_BLOCK))
        s = jnp.where(valid[None, :, None, None], s, -jnp.inf)
        lse = jax.nn.logsumexp(s, axis=-1)
        p = jnp.exp(s - lse[..., None])
        o = jnp.einsum('bnrhij,bnjrhd->bnirhd', p, vv)
        outs.append(o.reshape(B, s_pad, H, D)[:, :S])
        lses.append(lse.transpose(0, 1, 4, 2, 3).reshape(B, s_pad, H)[:, :S])
    return combine_dilated(outs, lses)


def dilated_attention_sample(q, k, v, cache_k, cache_v):
    B, T, H, D = q.shape
    L = cache_k.shape[1]
    q = q.astype(jnp.float32) * (D ** -0.5)
    k_all = jnp.concatenate([cache_k.astype(jnp.float32), k.astype(jnp.float32)], axis=1)
    v_all = jnp.concatenate([cache_v.astype(jnp.float32), v.astype(jnp.float32)], axis=1)
    outs, lses = [], []
    for window, dil in DILATED_PATTERNS:
        n_tap = window // dil
        idx = L + jnp.arange(T)[:, None] - dil * jnp.arange(n_tap + 1)[None, :]
        valid = idx >= 0
        idx = jnp.maximum(idx, 0)
        kg = jnp.take(k_all, idx, axis=1)
        vg = jnp.take(v_all, idx, axis=1)
        s = jnp.einsum('bthd,btjhd->bhtj', q, kg)
        s = jnp.where(valid[None, None], s, -jnp.inf)
        lse = jax.nn.logsumexp(s, axis=-1)
        p = jnp.exp(s - lse[..., None])
        outs.append(jnp.einsum('bhtj,btjhd->bthd', p, vg))
        lses.append(lse.transpose(0, 2, 1))
    return combine_dilated(outs, lses)


def gated_delta_rule(q, k, v, g, beta, s0):
    B, T, H, _ = q.shape
    C = min(GDN_CHUNK, T)
    tp = -(-T // C) * C
    n = tp // C

    def chunks(t):
        t = jnp.pad(t, [(0, 0), (0, tp - T)] + [(0, 0)] * (t.ndim - 2))
        t = t.reshape((B, n, C) + t.shape[2:])
        return jnp.moveaxis(t, (1, 3), (0, 2))

    qc, kc, vc, gc, bc = (chunks(t) for t in (q, k, v, g, beta))
    gcum = jnp.cumsum(gc, axis=-1)
    tri_incl = jnp.tril(jnp.ones((C, C), dtype=bool))
    tri_strict = jnp.tril(jnp.ones((C, C), dtype=bool), -1)
    decay = jnp.exp(jnp.where(tri_incl, gcum[..., :, None] - gcum[..., None, :], -jnp.inf))
    kbeta = kc * bc[..., None]
    a = jnp.where(tri_strict, jnp.einsum('nbhid,nbhjd->nbhij', kbeta, kc) * decay, 0.0)
    rhs = jnp.concatenate([vc * bc[..., None], kbeta * jnp.exp(gcum)[..., None]], axis=-1)
    sol = lax.linalg.triangular_solve(a + jnp.eye(C, dtype=a.dtype), rhs,
                                      left_side=True, lower=True, unit_diagonal=True)
    u, w = sol[..., :DV], sol[..., DV:]
    qk = jnp.einsum('nbhid,nbhjd->nbhij', qc, kc) * decay

    def step(S, xs):
        q_c, k_c, u_c, w_c, qk_c, g_c = xs
        v_new = u_c - jnp.einsum('bhck,bhkv->bhcv', w_c, S)
        o = (jnp.einsum('bhck,bhkv->bhcv', q_c * jnp.exp(g_c)[..., None], S)
             + jnp.einsum('bhij,bhjv->bhiv', qk_c, v_new))
        g_last = g_c[..., -1:]
        S = S * jnp.exp(g_last)[..., None] + jnp.einsum(
            'bhck,bhcv->bhkv', k_c * jnp.exp(g_last - g_c)[..., None], v_new)
        return S, o

    s_fin, o = lax.scan(step, s0, (qc, kc, u, w, qk, gcum))
    o = jnp.moveaxis(o, (0, 2), (1, 3)).reshape(B, tp, H, DV)[:, :T]
    return o, s_fin


def moe_ffn(h, w_router, b_router, w_gate_up, b_gate_up, w_down, b_down):
    B, T, D = h.shape
    n_tok = B * T
    n_asg = n_tok * TOP_K
    xt = h.reshape(n_tok, D)
    logits = (xt @ w_router + b_router).astype(jnp.float32)
    top_logit, top_e = lax.top_k(logits, TOP_K)
    top_w = jax.nn.softmax(top_logit, axis=-1)
    e_flat = top_e.reshape(-1)
    order = jnp.argsort(e_flat)
    e_sorted = e_flat[order]
    counts = jnp.bincount(e_flat, length=N_EXPERTS)
    padded = (counts + MOE_BLOCK - 1) // MOE_BLOCK * MOE_BLOCK
    pad_end = jnp.cumsum(padded)
    slot = (pad_end - padded)[e_sorted] + jnp.arange(n_asg) - (jnp.cumsum(counts) - counts)[e_sorted]
    n_blk = -(-n_asg // MOE_BLOCK) + N_EXPERTS
    slot_tok = jnp.zeros((n_blk * MOE_BLOCK,), jnp.int32).at[slot].set((order // TOP_K).astype(jnp.int32))
    slot_w = jnp.zeros((n_blk * MOE_BLOCK,), jnp.float32).at[slot].set(top_w.reshape(-1)[order])
    blk_start = jnp.arange(n_blk) * MOE_BLOCK
    blk_e = jnp.minimum(jnp.sum(pad_end[None, :] <= blk_start[:, None], axis=1), N_EXPERTS - 1)

    def expert_block(args):
        tok, e = args
        gu = xt[tok] @ w_gate_up[e] + b_gate_up[e]
        gate = jnp.minimum(gu[:, :D_FF], SWIGLU_LIMIT)
        up = jnp.clip(gu[:, D_FF:], -SWIGLU_LIMIT, SWIGLU_LIMIT)
        act = gate * jax.nn.sigmoid(SWIGLU_ALPHA * gate) * (up + 1.0)
        return act @ w_down[e] + b_down[e]

    y_blk = lax.map(expert_block, (slot_tok.reshape(n_blk, MOE_BLOCK), blk_e))
    out = jnp.zeros((n_tok, D), jnp.float32).at[slot_tok].add(
        slot_w[:, None] * y_blk.reshape(-1, D).astype(jnp.float32))
    return out.reshape(B, T, D).astype(h.dtype)


def trunk_layer(x, c, pos, win_k, win_v, conv_buf, ssm0,
                w_ada, b_ada, g_pre_mix, g_post_mix, g_pre_ffn, g_post_ffn,
                w_in, conv_w, a_log, dt_bias, g_gdn_norm, w_out,
                w_router, b_router, w_gate_up, b_gate_up, w_down, b_down):
    B, T, _ = x.shape
    mod = (jax.nn.silu(c) @ w_ada + b_ada)[:, None, :]
    shift1, scale1, gate1, shift2, scale2, gate2 = jnp.split(mod, 6, axis=-1)

    h = rmsnorm(x, g_pre_mix) * (1.0 + scale1) + shift1
    split_points = np.cumsum(PROJ_SPLITS)[:-1].tolist()
    q_a, k_a, v_a, qkv_b, z_b, beta_b, alpha_b = jnp.split(h @ w_in, split_points, axis=-1)

    q_a = rope(q_a.reshape(B, T, N_HEADS_A, HEAD_DIM), pos)
    k_a = rope(k_a.reshape(B, T, N_HEADS_A, HEAD_DIM), pos)
    v_a = v_a.reshape(B, T, N_HEADS_A, HEAD_DIM)
    if win_k is None:
        o_a = dilated_attention_prompt(q_a, k_a, v_a)
    else:
        o_a = dilated_attention_sample(q_a, k_a, v_a, win_k, win_v)

    xc = jnp.concatenate([conv_buf.astype(qkv_b.dtype), qkv_b], axis=1)
    conv = xc[:, 0:T] * conv_w[0]
    for i in range(1, CONV_W):
        conv = conv + xc[:, i:i + T] * conv_w[i]
    conv = jax.nn.silu(conv).astype(jnp.float32)
    new_conv = xc[:, T:]
    q_b, k_b, v_b = jnp.split(conv, [N_HEADS_B * DK, 2 * N_HEADS_B * DK], axis=-1)
    q_b = l2norm(q_b.reshape(B, T, N_HEADS_B, DK)) * (DK ** -0.5)
    k_b = l2norm(k_b.reshape(B, T, N_HEADS_B, DK))
    v_b = v_b.reshape(B, T, N_HEADS_B, DV)
    beta = jax.nn.sigmoid(beta_b.astype(jnp.float32))
    g = -jnp.exp(a_log.astype(jnp.float32)) * jax.nn.softplus(
        alpha_b.astype(jnp.float32) + dt_bias.astype(jnp.float32))
    o_b, new_ssm = gated_delta_rule(q_b, k_b, v_b, g, beta, ssm0.astype(jnp.float32))
    o_b = rmsnorm(o_b, g_gdn_norm) * jax.nn.silu(z_b.reshape(B, T, N_HEADS_B, DV).astype(jnp.float32))

    mix = jnp.concatenate([o_a.reshape(B, T, D_ATTN), o_b.reshape(B, T, D_DELTA)], axis=-1).astype(x.dtype) @ w_out
    x = x + gate1 * rmsnorm(mix, g_post_mix)

    h2 = rmsnorm(x, g_pre_ffn) * (1.0 + scale2) + shift2
    ff = moe_ffn(h2, w_router, b_router, w_gate_up, b_gate_up, w_down, b_down)
    x = x + gate2 * rmsnorm(ff, g_post_ffn)
    return x, k_a, v_a, new_conv, new_ssm


def setup_inputs(seed: int = 0) -> dict:
    key = jax.random.key(seed)
    ks = jax.random.split(key, 28)
    f32 = jnp.float32
    D = D_MODEL

    def nrm(k, shape, scale=1.0):
        return jax.random.normal(k, shape, f32) * scale

    win_buf = min(MAX_WINDOW, PAST_LEN)
    a_init = jax.random.uniform(ks[15], (DEPTH, N_HEADS_B), f32, 1.0, 16.0)
    dt = jnp.exp(jax.random.uniform(ks[16], (DEPTH, N_HEADS_B), f32, math.log(1e-3), math.log(1e-1)))
    return {
        'x_prompt': nrm(ks[0], (BATCH, SEQ, D)),
        'x_sample': nrm(ks[1], (DEC_BATCH, DEC_SEQ, D)),
        'c_prompt': nrm(ks[2], (BATCH, D)),
        'c_sample': nrm(ks[3], (DEC_BATCH, D)),
        'cache_win_k': nrm(ks[4], (DEPTH, DEC_BATCH, win_buf, N_HEADS_A, HEAD_DIM)),
        'cache_win_v': nrm(ks[5], (DEPTH, DEC_BATCH, win_buf, N_HEADS_A, HEAD_DIM)),
        'state_conv': nrm(ks[6], (DEPTH, DEC_BATCH, CONV_W - 1, CONV_DIM)),
        'state_ssm': nrm(ks[7], (DEPTH, DEC_BATCH, N_HEADS_B, DK, DV), 0.1),
        'w_ada': nrm(ks[8], (DEPTH, D, 6 * D), 0.5 * D ** -0.5),
        'b_ada': nrm(ks[9], (DEPTH, 6 * D), 0.02),
        'g_pre_mix': 1.0 + nrm(ks[10], (DEPTH, D), 0.1),
        'g_post_mix': 1.0 + nrm(ks[11], (DEPTH, D), 0.1),
        'g_pre_ffn': 1.0 + nrm(ks[12], (DEPTH, D), 0.1),
        'g_post_ffn': 1.0 + nrm(ks[13], (DEPTH, D), 0.1),
        'w_in': nrm(ks[14], (DEPTH, D, PROJ_DIM), D ** -0.5),
        'conv_w': nrm(ks[17], (DEPTH, CONV_W, CONV_DIM), CONV_W ** -0.5),
        'a_log': jnp.log(a_init),
        'dt_bias': dt + jnp.log(-jnp.expm1(-dt)),
        'g_gdn_norm': 1.0 + nrm(ks[18], (DEPTH, DV), 0.1),
        'w_out': nrm(ks[19], (DEPTH, D, D), D ** -0.5),
        'w_router': nrm(ks[20], (DEPTH, D, N_EXPERTS), D ** -0.5),
        'b_router': nrm(ks[21], (DEPTH, N_EXPERTS), 0.01),
        'w_gate_up': nrm(ks[22], (DEPTH, N_EXPERTS, D, 2 * D_FF), D ** -0.5),
        'b_gate_up': nrm(ks[23], (DEPTH, N_EXPERTS, 2 * D_FF), 0.01),
        'w_down': nrm(ks[24], (DEPTH, N_EXPERTS, D_FF, D), D_FF ** -0.5),
        'b_down': nrm(ks[25], (DEPTH, N_EXPERTS, D), 0.01),
    }


def reference(x_prompt, x_sample, c_prompt, c_sample, cache_win_k, cache_win_v, state_conv, state_ssm,
              w_ada, b_ada, g_pre_mix, g_post_mix, g_pre_ffn, g_post_ffn,
              w_in, conv_w, a_log, dt_bias, g_gdn_norm, w_out,
              w_router, b_router, w_gate_up, b_gate_up, w_down, b_down):
    b_p, t_p, _ = x_prompt.shape
    pos_p = jnp.arange(t_p)
    pos_s = PAST_LEN + jnp.arange(x_sample.shape[1])
    win_p = min(MAX_WINDOW, t_p)
    y_p, y_s = x_prompt, x_sample
    kp, vp, cp, sp, ks_, vs_, cs_, ss_ = ([] for _ in range(8))
    for layer in range(DEPTH):
        params = (w_ada[layer], b_ada[layer], g_pre_mix[layer], g_post_mix[layer], g_pre_ffn[layer],
                  g_post_ffn[layer], w_in[layer], conv_w[layer], a_log[layer], dt_bias[layer],
                  g_gdn_norm[layer], w_out[layer], w_router[layer], b_router[layer],
                  w_gate_up[layer], b_gate_up[layer], w_down[layer], b_down[layer])
        conv0 = jnp.zeros((b_p, CONV_W - 1, CONV_DIM), x_prompt.dtype)
        ssm0 = jnp.zeros((b_p, N_HEADS_B, DK, DV), jnp.float32)
        y_p, k_new, v_new, conv_new, ssm_new = trunk_layer(y_p, c_prompt, pos_p, None, None, conv0, ssm0, *params)
        kp.append(k_new[:, t_p - win_p:])
        vp.append(v_new[:, t_p - win_p:])
        cp.append(conv_new)
        sp.append(ssm_new)
        y_s, k_new, v_new, conv_new, ssm_new = trunk_layer(
            y_s, c_sample, pos_s, cache_win_k[layer], cache_win_v[layer], state_conv[layer], state_ssm[layer], *params)
        ks_.append(k_new)
        vs_.append(v_new)
        cs_.append(conv_new)
        ss_.append(ssm_new)
    new_win_k_prompt = jnp.stack(kp)
    new_win_v_prompt = jnp.stack(vp)
    new_conv_prompt = jnp.stack(cp)
    new_ssm_prompt = jnp.stack(sp)
    new_win_k_sample = jnp.stack(ks_)
    new_win_v_sample = jnp.stack(vs_)
    new_conv_sample = jnp.stack(cs_)
    new_ssm_sample = jnp.stack(ss_)
    return (y_p, y_s, new_win_k_prompt, new_win_v_prompt, new_conv_prompt, new_ssm_prompt,
            new_win_k_sample, new_win_v_sample, new_conv_sample, new_ssm_sample)
```

```python
import functools
import math

import numpy as np
import jax
import jax.numpy as jnp
from jax import lax
from jax.experimental import pallas as pl
from jax.experimental.pallas import tpu as pltpu

F32 = jnp.float32
BF16 = jnp.bfloat16

HEAD_DIM = 128
DILATED_PATTERNS = ((128, 1), (512, 4), (2048, 16))
ATTN_BLOCK = 128
PAST_LEN = 2048
ROPE_THETA = 10000.0
CONV_W = 4
N_EXPERTS = 32
TOP_K = 4
SWIGLU_LIMIT = 7.0
SWIGLU_ALPHA = 1.702
NORM_EPS = 1e-6
L2_EPS = 1e-6

LANES = 128
SUBLANES = 8
VMEM_LIMIT_BYTES = 56 * 1024 * 1024

NEG_BIG = -1e30

GDN_CHUNK = 128
MOE_BLOCK_ROWS = 1024
MOE_SUB_ROWS = 256
MOE_FF_TILE = 256


def _cparams(*sem):
    return pltpu.CompilerParams(dimension_semantics=sem, vmem_limit_bytes=VMEM_LIMIT_BYTES)


def _sigmoid(x):
    return 1.0 / (1.0 + jnp.exp(-x))


def _softplus(x):
    return jnp.maximum(x, 0.0) + jnp.log(1.0 + jnp.exp(-jnp.abs(x)))


def _dot(a, b):
    return jnp.dot(a, b, preferred_element_type=F32)


def _dot_nt(a, b):
    return lax.dot_general(a, b, (((1,), (1,)), ((), ())), preferred_element_type=F32)


def _split3(x):
    hi = x.astype(BF16)
    r = x - hi.astype(F32)
    mid = r.astype(BF16)
    lo = (r - mid.astype(F32)).astype(BF16)
    return hi, mid, lo


def _ada_kernel(c_ref, w_ref, b_ref, o_ref):
    c = c_ref[...]
    a = (c * _sigmoid(c)).astype(BF16)
    o_ref[...] = _dot(a, w_ref[...].astype(BF16)) + b_ref[...]


def ada_mod(c, w_ada, b_ada):
    rows, d = c.shape
    n = w_ada.shape[1]
    tn = 1024
    return pl.pallas_call(
        _ada_kernel,
        grid=(n // tn,),
        in_specs=[pl.BlockSpec((rows, d), lambda j: (0, 0)),
                  pl.BlockSpec((d, tn), lambda j: (0, j)),
                  pl.BlockSpec((1, tn), lambda j: (0, j))],
        out_specs=pl.BlockSpec((rows, tn), lambda j: (0, j)),
        out_shape=jax.ShapeDtypeStruct((rows, n), F32),
        compiler_params=_cparams("arbitrary"),
        name="ada_mod",
    )(c, w_ada, b_ada.reshape(1, n))


def _inproj_kernel(x_ref, sc_ref, sh_ref, g_ref, cos_ref, sin_ref, w_ref, wba_ref, o_ref, oba_ref, h_scr,
                   *, n_rope_tiles, n_q_tiles, heads_per_tile, q_scale, norm_rows_chunk):
    j = pl.program_id(1)

    @pl.when(j == 0)
    def _():
        wba = wba_ref[...].astype(BF16)
        per_row_mod = sc_ref.shape[0] != 1

        def norm_rows(ci, carry):
            rs = pl.ds(pl.multiple_of(ci * norm_rows_chunk, norm_rows_chunk), norm_rows_chunk)
            x = x_ref[rs, :]
            ms = jnp.mean(x * x, axis=-1, keepdims=True)
            y = x * lax.rsqrt(ms + NORM_EPS) * g_ref[...]
            sc = sc_ref[rs, :] if per_row_mod else sc_ref[...]
            sh = sh_ref[rs, :] if per_row_mod else sh_ref[...]
            hb = (y * (1.0 + sc) + sh).astype(BF16)
            h_scr[rs, :] = hb
            oba_ref[rs, :] = _dot(hb, wba)
            return carry
        lax.fori_loop(0, x_ref.shape[0] // norm_rows_chunk, norm_rows, 0)

    acc = _dot(h_scr[...], w_ref[...].astype(BF16))

    @pl.when(j < n_rope_tiles)
    def _():
        scale = jnp.where(j < n_q_tiles, q_scale, 1.0).astype(F32)
        cos = cos_ref[...] * scale
        sin = sin_ref[...] * scale
        for h in range(heads_per_tile):
            xh = acc[:, h * HEAD_DIM:(h + 1) * HEAD_DIM]
            o_ref[:, h * HEAD_DIM:(h + 1) * HEAD_DIM] = xh * cos + pltpu.roll(xh, HEAD_DIM // 2, 1) * sin

    @pl.when(j >= n_rope_tiles)
    def _():
        o_ref[...] = acc


def in_proj(x, scale, shift, g, cos, sin, w_in, w_ba, *, tm, d_attn):
    rows, d = x.shape
    tm = min(tm, rows)
    tn = 512
    n_main = (w_in.shape[1] // tn) * tn
    mod_rows = scale.shape[0]
    mod_block = (tm, d) if mod_rows == rows else (1, d)
    mod_map = (lambda i, j: (i, 0)) if mod_rows == rows else (lambda i, j: (0, 0))
    kern = functools.partial(_inproj_kernel, n_rope_tiles=2 * d_attn // tn, n_q_tiles=d_attn // tn,
                             heads_per_tile=tn // HEAD_DIM, q_scale=HEAD_DIM ** -0.5, norm_rows_chunk=min(256, tm))
    return pl.pallas_call(
        kern,
        grid=(rows // tm, n_main // tn),
        in_specs=[pl.BlockSpec((tm, d), lambda i, j: (i, 0)),
                  pl.BlockSpec(mod_block, mod_map),
                  pl.BlockSpec(mod_block, mod_map),
                  pl.BlockSpec((1, d), lambda i, j: (0, 0)),
                  pl.BlockSpec((tm, HEAD_DIM), lambda i, j: (i, 0)),
                  pl.BlockSpec((tm, HEAD_DIM), lambda i, j: (i, 0)),
                  pl.BlockSpec((d, tn), lambda i, j: (0, j)),
                  pl.BlockSpec((d, LANES), lambda i, j: (0, 0))],
        out_specs=[pl.BlockSpec((tm, tn), lambda i, j: (i, j)),
                   pl.BlockSpec((tm, LANES), lambda i, j: (i, 0))],
        out_shape=[jax.ShapeDtypeStruct((rows, n_main), F32),
                   jax.ShapeDtypeStruct((rows, LANES), F32)],
        scratch_shapes=[pltpu.VMEM((tm, d), BF16)],
        compiler_params=_cparams("arbitrary", "arbitrary"),
        name="in_proj",
    )(x, scale, shift, g.reshape(1, d), cos, sin, w_in, w_ba)


def _attn_prompt_kernel(q_ref, kp_ref, kc_ref, vp_ref, vc_ref, o_ref, l_ref, *, n_heads):
    n = pl.program_id(1)
    qi = lax.broadcasted_iota(jnp.int32, (ATTN_BLOCK, ATTN_BLOCK), 0)
    kj = lax.broadcasted_iota(jnp.int32, (ATTN_BLOCK, ATTN_BLOCK), 1)
    mask_prev = jnp.logical_and(kj >= qi, n > 0)
    mask_cur = kj <= qi
    for h in range(n_heads):
        sl = slice(h * HEAD_DIM, (h + 1) * HEAD_DIM)
        q = q_ref[:, sl].astype(BF16)
        s_p = jnp.where(mask_prev, _dot_nt(q, kp_ref[:, sl].astype(BF16)), NEG_BIG)
        s_c = jnp.where(mask_cur, _dot_nt(q, kc_ref[:, sl].astype(BF16)), NEG_BIG)
        m = jnp.maximum(jnp.max(s_p, axis=-1, keepdims=True), jnp.max(s_c, axis=-1, keepdims=True))
        p_p = jnp.exp(s_p - m)
        p_c = jnp.exp(s_c - m)
        l = jnp.sum(p_p, axis=-1, keepdims=True) + jnp.sum(p_c, axis=-1, keepdims=True)
        o = _dot(p_p.astype(BF16), vp_ref[:, sl].astype(BF16)) + _dot(p_c.astype(BF16), vc_ref[:, sl].astype(BF16))
        o_ref[:, sl] = o / l
        l_ref[:, sl] = jnp.broadcast_to(m + jnp.log(l), (ATTN_BLOCK, HEAD_DIM))


def attn_prompt_pattern(p, dil, *, d_attn, n_col_groups):
    s, width = p.shape
    rows = s // dil
    pv = p.reshape(rows, dil * width)
    nb = rows // ATTN_BLOCK
    g = n_col_groups
    blk = (ATTN_BLOCK, d_attn)
    kern = functools.partial(_attn_prompt_kernel, n_heads=d_attn // HEAD_DIM)
    o, lse = pl.pallas_call(
        kern,
        grid=(dil, nb),
        in_specs=[pl.BlockSpec(blk, lambda r, n: (n, r * g)),
                  pl.BlockSpec(blk, lambda r, n: (jnp.maximum(n - 1, 0), r * g + 1)),
                  pl.BlockSpec(blk, lambda r, n: (n, r * g + 1)),
                  pl.BlockSpec(blk, lambda r, n: (jnp.maximum(n - 1, 0), r * g + 2)),
                  pl.BlockSpec(blk, lambda r, n: (n, r * g + 2))],
        out_specs=[pl.BlockSpec(blk, lambda r, n: (n, r)),
                   pl.BlockSpec(blk, lambda r, n: (n, r))],
        out_shape=[jax.ShapeDtypeStruct((rows, dil * d_attn), F32),
                   jax.ShapeDtypeStruct((rows, dil * d_attn), F32)],
        compiler_params=_cparams("arbitrary", "arbitrary"),
        name=f"attn_prompt_d{dil}",
    )(pv, pv, pv, pv, pv)
    return o.reshape(s, d_attn), lse.reshape(s, d_attn)


def _attn_sample_kernel(q_ref, kn_ref, vn_ref, ck_ref, cv_ref, cnt_ref, o_ref, *, n_heads, t_new):
    cnt = cnt_ref[...]
    has = cnt > 0.0
    ti = lax.broadcasted_iota(jnp.int32, (t_new, 1), 0)
    for h in range(n_heads):
        sl = slice(h * HEAD_DIM, (h + 1) * HEAD_DIM)
        q = q_ref[:, sl]
        kn = kn_ref[:, sl]
        vn = vn_ref[:, sl]
        s_c = jnp.where(has, _dot_nt(q.astype(BF16), ck_ref[0, :, sl].astype(BF16)), NEG_BIG)
        m = jnp.max(s_c, axis=-1, keepdims=True)
        s_n, c_n = [], []
        for t2 in range(t_new):
            diff = ti - t2
            c = jnp.zeros((t_new, 1), F32)
            for window, dil in DILATED_PATTERNS:
                ok = (diff >= 0) & ((diff & (dil - 1)) == 0) & (diff <= window)
                c = c + jnp.where(ok, 1.0, 0.0)
            sn = jnp.where(c > 0.0, jnp.sum(q * kn[t2:t2 + 1, :], axis=-1, keepdims=True), NEG_BIG)
            m = jnp.maximum(m, sn)
            s_n.append(sn)
            c_n.append(c)
        e_c = cnt * jnp.exp(s_c - m)
        l = jnp.sum(e_c, axis=-1, keepdims=True)
        acc = _dot(e_c.astype(BF16), cv_ref[0, :, sl].astype(BF16))
        for t2 in range(t_new):
            e_n = c_n[t2] * jnp.exp(s_n[t2] - m)
            l = l + e_n
            acc = acc + e_n * vn[t2:t2 + 1, :]
        o_ref[:, sl] = acc / l


def _sample_key_counts(t_new, cache_len):
    t = np.arange(t_new)[:, None]
    p = np.arange(cache_len)[None, :]
    diff = cache_len + t - p
    cnt = np.zeros((t_new, cache_len), np.float32)
    for window, dil in DILATED_PATTERNS:
        cnt += ((diff % dil == 0) & (diff <= window)).astype(np.float32)
    return cnt


def attn_sample(p, cache_k, cache_v, *, d_attn, t_new):
    bt = p.shape[0]
    b = bt // t_new
    cache_len = cache_k.shape[1]
    assert cache_len >= max(w for w, _ in DILATED_PATTERNS)
    for _, dil in DILATED_PATTERNS:
        assert dil & (dil - 1) == 0
    cnt = jnp.asarray(_sample_key_counts(t_new, cache_len))
    kern = functools.partial(_attn_sample_kernel, n_heads=d_attn // HEAD_DIM, t_new=t_new)
    blk = (t_new, d_attn)
    return pl.pallas_call(
        kern,
        grid=(b,),
        in_specs=[pl.BlockSpec(blk, lambda i: (i, 0)),
                  pl.BlockSpec(blk, lambda i: (i, 1)),
                  pl.BlockSpec(blk, lambda i: (i, 2)),
                  pl.BlockSpec((1, cache_len, d_attn), lambda i: (i, 0, 0)),
                  pl.BlockSpec((1, cache_len, d_attn), lambda i: (i, 0, 0)),
                  pl.BlockSpec((t_new, cache_len), lambda i: (0, 0))],
        out_specs=pl.BlockSpec(blk, lambda i: (i, 0)),
        out_shape=jax.ShapeDtypeStruct((bt, d_attn), F32),
        compiler_params=_cparams("arbitrary"),
        name="attn_sample",
    )(p, p, p, cache_k, cache_v, cnt)


def _causal_conv_silu(x, prev8, cw):
    t = x.shape[0]
    row8 = lax.broadcasted_iota(jnp.int32, (SUBLANES, 1), 0)
    conv = x * cw[CONV_W - 1:CONV_W, :]
    for i in range(1, CONV_W):
        xr = pltpu.roll(x, i, 0)
        pr = pltpu.roll(prev8, i, 0)
        head = jnp.where(row8 < i, pr, xr[:SUBLANES, :])
        xs = head if t == SUBLANES else jnp.concatenate([head, xr[SUBLANES:, :]], axis=0)
        conv = conv + xs * cw[CONV_W - 1 - i:CONV_W - i, :]
    return conv * _sigmoid(conv)


def _l2norm(x):
    return x * lax.rsqrt(jnp.sum(x * x, axis=-1, keepdims=True) + L2_EPS)


def _gated_rmsnorm(o, g_row, z):
    y = o * lax.rsqrt(jnp.mean(o * o, axis=-1, keepdims=True) + NORM_EPS) * g_row
    return y * (z * _sigmoid(z))


def _beta_and_g(ba, alog_row, dtb_row):
    beta = _sigmoid(ba)
    g = -jnp.exp(alog_row) * _softplus(ba + dtb_row)
    return beta, g


def _gdn_prompt_kernel(qkv_ref, z_ref, ba_ref, cw_ref, alog_ref, dtb_ref, gn_ref,
                       o_ref, tail_ref, ssm_ref, s_scr, prev_scr, *, n_heads):
    n = pl.program_id(0)
    c = GDN_CHUNK
    hd = HEAD_DIM

    @pl.when(n == 0)
    def _():
        s_scr[...] = jnp.zeros_like(s_scr)
        prev_scr[...] = jnp.zeros_like(prev_scr)

    x = qkv_ref[...]
    conv = _causal_conv_silu(x, prev_scr[...], cw_ref[...])
    prev_scr[...] = x[c - SUBLANES:, :]
    tail_ref[...] = x[c - SUBLANES:, :]

    beta_all, g_all = _beta_and_g(ba_ref[...], alog_ref[...], dtb_ref[...])
    ri = lax.broadcasted_iota(jnp.int32, (c, c), 0)
    ci = lax.broadcasted_iota(jnp.int32, (c, c), 1)
    tri_incl = ri >= ci
    tri_strict = ri > ci
    eye = jnp.where(ri == ci, 1.0, 0.0).astype(F32)
    blk_xor = ri ^ ci
    tri_b = jnp.where(tri_incl, 1.0, 0.0).astype(BF16)
    g_hi, g_mid, g_lo = _split3(g_all)
    gcum = _dot(tri_b, g_hi) + _dot(tri_b, g_mid) + _dot(tri_b, g_lo)
    gcum_t = gcum.T
    gn = gn_ref[...]

    for h in range(n_heads):
        q = _l2norm(conv[:, h * hd:(h + 1) * hd]) * (hd ** -0.5)
        k = _l2norm(conv[:, (n_heads + h) * hd:(n_heads + h + 1) * hd])
        v = conv[:, (2 * n_heads + h) * hd:(2 * n_heads + h + 1) * hd]
        beta = beta_all[:, h:h + 1]
        gc = gcum[:, n_heads + h:n_heads + h + 1]
        gr = gcum_t[n_heads + h:n_heads + h + 1, :]
        g_last = gcum[c - 1:c, n_heads + h:n_heads + h + 1]
        decay = jnp.where(tri_incl, jnp.exp(gc - gr), 0.0)
        kb = k * beta
        k16 = k.astype(BF16)
        a = jnp.where(tri_strict, _dot_nt(kb.astype(BF16), k16) * decay, 0.0)
        t_inv = eye - jnp.where((blk_xor >> 1) == 0, a, 0.0)
        for lvl in range(2, int(math.log2(c)) + 1):
            l16 = jnp.where((blk_xor >> (lvl - 1)) == 1, a, 0.0).astype(BF16)
            t16 = t_inv.astype(BF16)
            t_inv = t_inv - _dot(t16, _dot(l16, t16).astype(BF16))
        rhs = jnp.concatenate([v * beta, kb * jnp.exp(gc)], axis=-1)
        sol = _dot(t_inv.astype(BF16), rhs.astype(BF16))
        u = sol[:, :hd]
        w = sol[:, hd:]
        qk = _dot_nt(q.astype(BF16), k16) * decay

        s_old = s_scr[h]
        s16 = s_old.astype(BF16)
        v_new = u - _dot(w.astype(BF16), s16)
        v16 = v_new.astype(BF16)
        o = _dot((q * jnp.exp(gc)).astype(BF16), s16) + _dot(qk.astype(BF16), v16)
        k_dec = (k * jnp.exp(g_last - gc)).T
        s_scr[h] = s_old * jnp.exp(g_last) + _dot(k_dec.astype(BF16), v16)
        o_ref[:, h * hd:(h + 1) * hd] = _gated_rmsnorm(o, gn, z_ref[:, h * hd:(h + 1) * hd])

    @pl.when(n == pl.num_programs(0) - 1)
    def _():
        ssm_ref[...] = s_scr[...]


def gdn_prompt(p, pba, conv_w, alog_row, dtb_row, g_norm, *, n_heads, qkv_col_block, z_col_block):
    s = p.shape[0]
    c = GDN_CHUNK
    hd = HEAD_DIM
    conv_dim = 3 * n_heads * hd
    kern = functools.partial(_gdn_prompt_kernel, n_heads=n_heads)
    return pl.pallas_call(
        kern,
        grid=(s // c,),
        in_specs=[pl.BlockSpec((c, conv_dim), lambda n: (n, qkv_col_block)),
                  pl.BlockSpec((c, n_heads * hd), lambda n: (n, z_col_block)),
                  pl.BlockSpec((c, LANES), lambda n: (n, 0)),
                  pl.BlockSpec((CONV_W, conv_dim), lambda n: (0, 0)),
                  pl.BlockSpec((1, LANES), lambda n: (0, 0)),
                  pl.BlockSpec((1, LANES), lambda n: (0, 0)),
                  pl.BlockSpec((1, hd), lambda n: (0, 0))],
        out_specs=[pl.BlockSpec((c, n_heads * hd), lambda n: (n, 0)),
                   pl.BlockSpec((SUBLANES, conv_dim), lambda n: (0, 0)),
                   pl.BlockSpec((n_heads, hd, hd), lambda n: (0, 0, 0))],
        out_shape=[jax.ShapeDtypeStruct((s, n_heads * hd), F32),
                   jax.ShapeDtypeStruct((SUBLANES, conv_dim), F32),
                   jax.ShapeDtypeStruct((n_heads, hd, hd), F32)],
        scratch_shapes=[pltpu.VMEM((n_heads, hd, hd), F32), pltpu.VMEM((SUBLANES, conv_dim), F32)],
        compiler_params=_cparams("arbitrary"),
        name="gdn_prompt",
    )(p, p, pba, conv_w, alog_row, dtb_row, g_norm.reshape(1, hd))


def _gdn_sample_kernel(qkv_ref, z_ref, ba_ref, prev_ref, s0_ref, cw_ref, alog_ref, dtb_ref, gn_ref,
                       o_ref, ssm_ref, *, n_heads, t_new):
    hd = HEAD_DIM
    x = qkv_ref[...]
    conv = _causal_conv_silu(x, prev_ref[0], cw_ref[...])
    beta_all, g_all = _beta_and_g(ba_ref[...], alog_ref[...], dtb_ref[...])
    eg_all = jnp.exp(g_all)
    gn = gn_ref[...]
    pad = jnp.zeros((hd - 2 * t_new, hd), F32)
    for h in range(n_heads):
        q = _l2norm(conv[:, h * hd:(h + 1) * hd]) * (hd ** -0.5)
        k = _l2norm(conv[:, (n_heads + h) * hd:(n_heads + h + 1) * hd])
        v = conv[:, (2 * n_heads + h) * hd:(2 * n_heads + h + 1) * hd]
        kq_t = jnp.concatenate([k, q, pad], axis=0).T
        s = s0_ref[0, h]
        rows = []
        for t in range(t_new):
            kb = jnp.broadcast_to(kq_t[:, t:t + 1], (hd, hd))
            qb = jnp.broadcast_to(kq_t[:, t_new + t:t_new + t + 1], (hd, hd))
            eg = eg_all[t:t + 1, n_heads + h:n_heads + h + 1]
            bt = beta_all[t:t + 1, h:h + 1]
            s = s * eg
            ks = jnp.sum(s * kb, axis=0, keepdims=True)
            v_new = bt * (v[t:t + 1, :] - ks)
            s = s + kb * v_new
            rows.append(jnp.sum(s * qb, axis=0, keepdims=True))
        ssm_ref[0, h] = s
        o = jnp.concatenate(rows, axis=0)
        o_ref[:, h * hd:(h + 1) * hd] = _gated_rmsnorm(o, gn, z_ref[:, h * hd:(h + 1) * hd])


def gdn_sample(p, pba, prev8, s0, conv_w, alog_row, dtb_row, g_norm, *, n_heads, t_new, qkv_col_block, z_col_block):
    bt = p.shape[0]
    b = bt // t_new
    hd = HEAD_DIM
    conv_dim = 3 * n_heads * hd
    kern = functools.partial(_gdn_sample_kernel, n_heads=n_heads, t_new=t_new)
    return pl.pallas_call(
        kern,
        grid=(b,),
        in_specs=[pl.BlockSpec((t_new, conv_dim), lambda i: (i, qkv_col_block)),
                  pl.BlockSpec((t_new, n_heads * hd), lambda i: (i, z_col_block)),
                  pl.BlockSpec((t_new, LANES), lambda i: (i, 0)),
                  pl.BlockSpec((1, SUBLANES, conv_dim), lambda i: (i, 0, 0)),
                  pl.BlockSpec((1, n_heads, hd, hd), lambda i: (i, 0, 0, 0)),
                  pl.BlockSpec((CONV_W, conv_dim), lambda i: (0, 0)),
                  pl.BlockSpec((1, LANES), lambda i: (0, 0)),
                  pl.BlockSpec((1, LANES), lambda i: (0, 0)),
                  pl.BlockSpec((1, hd), lambda i: (0, 0))],
        out_specs=[pl.BlockSpec((t_new, n_heads * hd), lambda i: (i, 0)),
                   pl.BlockSpec((1, n_heads, hd, hd), lambda i: (i, 0, 0, 0))],
        out_shape=[jax.ShapeDtypeStruct((bt, n_heads * hd), F32),
                   jax.ShapeDtypeStruct((b, n_heads, hd, hd), F32)],
        compiler_params=_cparams("arbitrary"),
        name="gdn_sample",
    )(p, p, pba, prev8, s0, conv_w, alog_row, dtb_row, g_norm.reshape(1, hd))


def _outproj_kernel(*refs, n_pat):
    oa_refs = refs[:n_pat]
    ls_refs = refs[n_pat:2 * n_pat] if n_pat > 1 else ()
    k0 = 2 * n_pat if n_pat > 1 else 1
    (ob_ref, x_ref, gate_ref, sc_ref, sh_ref, gpost_ref, gpre_ref, wo_ref, wr_hi_ref, wr_lo_ref, br_ref,
     x1_ref, h2_ref, tw_ref, ti_ref) = refs[k0:]

    if n_pat > 1:
        m = ls_refs[0][...]
        for r in ls_refs[1:]:
            m = jnp.maximum(m, r[...])
        num = jnp.zeros_like(m)
        den = jnp.zeros_like(m)
        for o_r, l_r in zip(oa_refs, ls_refs):
            wgt = jnp.exp(l_r[...] - m)
            num = num + wgt * o_r[...]
            den = den + wgt
        o_a = num / den
    else:
        o_a = oa_refs[0][...]

    d_attn = o_a.shape[1]
    mix = (_dot(o_a.astype(BF16), wo_ref[:d_attn, :]) + _dot(ob_ref[...].astype(BF16), wo_ref[d_attn:, :]))
    y = mix * lax.rsqrt(jnp.mean(mix * mix, axis=-1, keepdims=True) + NORM_EPS) * gpost_ref[...]
    x1 = x_ref[...] + gate_ref[...] * y
    x1_ref[...] = x1
    hn = x1 * lax.rsqrt(jnp.mean(x1 * x1, axis=-1, keepdims=True) + NORM_EPS) * gpre_ref[...]
    h2 = hn * (1.0 + sc_ref[...]) + sh_ref[...]
    h2_ref[...] = h2.astype(BF16)

    h_hi = h2.astype(BF16)
    h_lo = (h2 - h_hi.astype(F32)).astype(BF16)
    logits = (_dot(h_hi, wr_hi_ref[...]) + _dot(h_hi, wr_lo_ref[...]) + _dot(h_lo, wr_hi_ref[...])) + br_ref[...]
    lane = lax.broadcasted_iota(jnp.int32, logits.shape, 1)
    tops, idxs = [], []
    for _ in range(TOP_K):
        mx = jnp.max(logits, axis=-1, keepdims=True)
        ix = jnp.min(jnp.where(logits == mx, lane, LANES), axis=-1, keepdims=True)
        tops.append(mx)
        idxs.append(ix)
        logits = jnp.where(lane == ix, 2.0 * NEG_BIG, logits)
    es = [jnp.exp(t - tops[0]) for t in tops]
    tot = es[0]
    for e in es[1:]:
        tot = tot + e
    tw = jnp.zeros(logits.shape, F32)
    ti = jnp.zeros(logits.shape, jnp.int32)
    for kk in range(TOP_K):
        tw = jnp.where(lane == kk, es[kk] / tot, tw)
        ti = jnp.where(lane == kk, idxs[kk], ti)
    tw_ref[...] = tw
    ti_ref[...] = ti


def out_proj(o_as, lses, o_b, x, gate, scale, shift, g_post, g_pre, wo16, wr_hi, wr_lo, br_row, *, tm):
    rows, d = x.shape
    tm = min(tm, rows)
    d_attn = o_as[0].shape[1]
    n_pat = len(o_as)
    mod_rows = gate.shape[0]
    mod_block = (tm, d) if mod_rows == rows else (1, d)
    mod_map = (lambda i: (i, 0)) if mod_rows == rows else (lambda i: (0, 0))
    half = pl.BlockSpec((tm, d_attn), lambda i: (i, 0))
    full = pl.BlockSpec((tm, d), lambda i: (i, 0))
    row_d = pl.BlockSpec((1, d), lambda i: (0, 0))
    lane_blk = pl.BlockSpec((tm, LANES), lambda i: (i, 0))
    in_specs = ([half] * n_pat + ([half] * n_pat if n_pat > 1 else [])
                + [pl.BlockSpec((tm, d - d_attn), lambda i: (i, 0)), full,
                   pl.BlockSpec(mod_block, mod_map), pl.BlockSpec(mod_block, mod_map), pl.BlockSpec(mod_block, mod_map),
                   row_d, row_d,
                   pl.BlockSpec((d, d), lambda i: (0, 0)),
                   pl.BlockSpec((d, LANES), lambda i: (0, 0)), pl.BlockSpec((d, LANES), lambda i: (0, 0)),
                   pl.BlockSpec((1, LANES), lambda i: (0, 0))])
    args = list(o_as) + (list(lses) if n_pat > 1 else []) + [
        o_b, x, gate, scale, shift, g_post.reshape(1, d), g_pre.reshape(1, d), wo16, wr_hi, wr_lo, br_row]
    return pl.pallas_call(
        functools.partial(_outproj_kernel, n_pat=n_pat),
        grid=(rows // tm,),
        in_specs=in_specs,
        out_specs=[full, full, lane_blk, lane_blk],
        out_shape=[jax.ShapeDtypeStruct((rows, d), F32), jax.ShapeDtypeStruct((rows, d), BF16),
                   jax.ShapeDtypeStruct((rows, LANES), F32), jax.ShapeDtypeStruct((rows, LANES), jnp.int32)],
        compiler_params=_cparams("arbitrary"),
        name=f"out_proj_{n_pat}",
    )(*args)


def _moe_kernel(e_ref, ns_ref, x_ref, wg_ref, wu_ref, bg_ref, bu_ref, wd_ref, bd_ref, o_ref):
    w = pl.program_id(0)
    j = pl.program_id(1)
    ns = ns_ref[w]
    sb = MOE_SUB_ROWS
    n_sub = MOE_BLOCK_ROWS // sb

    @pl.when(j == 0)
    def _():
        def zero(s, carry):
            r0 = pl.multiple_of(s * sb, sb)
            o_ref[pl.ds(r0, sb), :] = jnp.zeros((sb, o_ref.shape[1]), F32)
            return carry
        lax.fori_loop(ns, n_sub, zero, 0)

    @pl.when(ns > 0)
    def _():
        wg = wg_ref[0].astype(BF16)
        wu = wu_ref[0].astype(BF16)
        wd = wd_ref[0].astype(BF16)
        bg = bg_ref[0]
        bu = bu_ref[0]
        bd = bd_ref[0]

        def body(s, carry):
            r0 = pl.multiple_of(s * sb, sb)
            xs = x_ref[pl.ds(r0, sb), :]
            gate = jnp.minimum(_dot(xs, wg) + bg, SWIGLU_LIMIT)
            up = jnp.clip(_dot(xs, wu) + bu, -SWIGLU_LIMIT, SWIGLU_LIMIT)
            act = gate * _sigmoid(SWIGLU_ALPHA * gate) * (up + 1.0)
            y = _dot(act.astype(BF16), wd)

            @pl.when(j == 0)
            def _():
                o_ref[pl.ds(r0, sb), :] = y + bd

            @pl.when(j > 0)
            def _():
                o_ref[pl.ds(r0, sb), :] += y
            return carry
        lax.fori_loop(0, ns, body, 0)


def moe_experts(x_sorted, item_e, item_ns, w_gate_up, b_gate_up, w_down, b_down):
    n_rows, d = x_sorted.shape
    n_e, _, two_ff = w_gate_up.shape
    d_ff = two_ff // 2
    bm = MOE_BLOCK_ROWS
    tf = MOE_FF_TILE
    n_f = d_ff // tf
    n_items = n_rows // bm

    def jj(j, ns, w):
        return jnp.where(ns[w] > 0, j, n_f - 1)

    grid_spec = pltpu.PrefetchScalarGridSpec(
        num_scalar_prefetch=2,
        grid=(n_items, n_f),
        in_specs=[pl.BlockSpec((bm, d), lambda w, j, e, ns: (w, 0)),
                  pl.BlockSpec((1, d, tf), lambda w, j, e, ns: (e[w], 0, jj(j, ns, w))),
                  pl.BlockSpec((1, d, tf), lambda w, j, e, ns: (e[w], 0, n_f + jj(j, ns, w))),
                  pl.BlockSpec((1, 1, tf), lambda w, j, e, ns: (e[w], 0, jj(j, ns, w))),
                  pl.BlockSpec((1, 1, tf), lambda w, j, e, ns: (e[w], 0, n_f + jj(j, ns, w))),
                  pl.BlockSpec((1, tf, d), lambda w, j, e, ns: (e[w], jj(j, ns, w), 0)),
                  pl.BlockSpec((1, 1, d), lambda w, j, e, ns: (e[w], 0, 0))],
        out_specs=pl.BlockSpec((bm, d), lambda w, j, e, ns: (w, 0)),
    )
    return pl.pallas_call(
        _moe_kernel,
        grid_spec=grid_spec,
        out_shape=jax.ShapeDtypeStruct((n_rows, d), F32),
        compiler_params=_cparams("arbitrary", "arbitrary"),
        name="moe_experts",
    )(item_e, item_ns, x_sorted, w_gate_up, w_gate_up, b_gate_up.reshape(n_e, 1, two_ff),
      b_gate_up.reshape(n_e, 1, two_ff), w_down, b_down.reshape(n_e, 1, d))


def moe_routing(top_e):
    n_tok = top_e.shape[0]
    n_asg = n_tok * TOP_K
    bm = MOE_BLOCK_ROWS
    n_items = n_asg // bm + N_EXPERTS
    e_flat = top_e.reshape(-1)
    order = jnp.argsort(e_flat).astype(jnp.int32)
    e_sorted = e_flat[order]
    counts = jnp.bincount(e_flat, length=N_EXPERTS).astype(jnp.int32)
    padded = (counts + bm - 1) // bm * bm
    pad_end = jnp.cumsum(padded)
    seg_start = pad_end - padded
    cnt_start = jnp.cumsum(counts) - counts
    slot = seg_start[e_sorted] + jnp.arange(n_asg, dtype=jnp.int32) - cnt_start[e_sorted]
    slot_tok = jnp.zeros((n_items * bm,), jnp.int32).at[slot].set(order // TOP_K)
    pos = jnp.zeros((n_asg,), jnp.int32).at[order].set(slot).reshape(n_tok, TOP_K)
    item_start = jnp.arange(n_items, dtype=jnp.int32) * bm
    item_e_raw = jnp.minimum(jnp.sum(pad_end[None, :] <= item_start[:, None], axis=1), N_EXPERTS - 1).astype(jnp.int32)
    rows_left = jnp.where(item_start < pad_end[-1], counts[item_e_raw] - (item_start - seg_start[item_e_raw]), 0)
    item_ns = (jnp.clip(rows_left, 0, bm) + MOE_SUB_ROWS - 1) // MOE_SUB_ROWS
    last_e = item_e_raw[jnp.maximum(pad_end[-1] // bm - 1, 0)]
    item_e = jnp.where(item_start < pad_end[-1], item_e_raw, last_e)
    return slot_tok, pos, item_e.astype(jnp.int32), item_ns.astype(jnp.int32)


def _finish_kernel(x1_ref, yk_ref, tw_ref, gate_ref, g_ref, o_ref):
    d = x1_ref.shape[1]
    tw = tw_ref[...]
    ff = tw[:, 0:1] * yk_ref[:, 0:d]
    for kk in range(1, TOP_K):
        ff = ff + tw[:, kk:kk + 1] * yk_ref[:, kk * d:(kk + 1) * d]
    y = ff * lax.rsqrt(jnp.mean(ff * ff, axis=-1, keepdims=True) + NORM_EPS) * g_ref[...]
    o_ref[...] = x1_ref[...] + gate_ref[...] * y


def finish(x1, yk, tw, gate, g_post, *, tm):
    rows, d = x1.shape
    tm = min(tm, rows)
    mod_rows = gate.shape[0]
    mod_block = (tm, d) if mod_rows == rows else (1, d)
    mod_map = (lambda i: (i, 0)) if mod_rows == rows else (lambda i: (0, 0))
    return pl.pallas_call(
        _finish_kernel,
        grid=(rows // tm,),
        in_specs=[pl.BlockSpec((tm, d), lambda i: (i, 0)),
                  pl.BlockSpec((tm, TOP_K * d), lambda i: (i, 0)),
                  pl.BlockSpec((tm, LANES), lambda i: (i, 0)),
                  pl.BlockSpec(mod_block, mod_map),
                  pl.BlockSpec((1, d), lambda i: (0, 0))],
        out_specs=pl.BlockSpec((tm, d), lambda i: (i, 0)),
        out_shape=jax.ShapeDtypeStruct((rows, d), F32),
        compiler_params=_cparams("arbitrary"),
        name="finish",
    )(x1, yk, tw, gate, g_post.reshape(1, d))


def _rope_tables(pos):
    half = HEAD_DIM // 2
    inv_freq = ROPE_THETA ** (-jnp.arange(half, dtype=F32) / half)
    ang = pos.astype(F32)[:, None] * inv_freq[None, :]
    cos = jnp.cos(ang)
    sin = jnp.sin(ang)
    return jnp.concatenate([cos, cos], axis=-1), jnp.concatenate([-sin, sin], axis=-1)


def _lane_row(vals, offset):
    return jnp.zeros((1, LANES), F32).at[0, offset:offset + vals.shape[0]].set(vals.astype(F32))


def kernel(x_prompt, x_sample, c_prompt, c_sample, cache_win_k, cache_win_v, state_conv, state_ssm, w_ada, b_ada, g_pre_mix, g_post_mix, g_pre_ffn, g_post_ffn, w_in, conv_w, a_log, dt_bias, g_gdn_norm, w_out, w_router, b_router, w_gate_up, b_gate_up, w_down, b_down):
    depth = w_ada.shape[0]
    assert depth == 1 and x_prompt.shape[0] == 1
    _, s_len, d = x_prompt.shape
    b_s, t_s, _ = x_sample.shape
    n_heads_a = cache_win_k.shape[3]
    d_attn = n_heads_a * HEAD_DIM
    n_heads_b = state_ssm.shape[2]
    conv_dim = 3 * n_heads_b * HEAD_DIM
    n_samp = b_s * t_s
    win_p = min(max(w for w, _ in DILATED_PATTERNS), s_len)

    c_all = jnp.concatenate([c_prompt, c_sample], axis=0)
    c_rows = -(-c_all.shape[0] // SUBLANES) * SUBLANES
    c_all = jnp.pad(c_all, ((0, c_rows - c_all.shape[0]), (0, 0)))
    mod = ada_mod(c_all, w_ada[0], b_ada[0])
    mod_p = [mod[0:1, i * d:(i + 1) * d] for i in range(6)]
    mod_s = [jnp.repeat(mod[1:1 + b_s, i * d:(i + 1) * d], t_s, axis=0) for i in range(6)]

    n_main = d_attn * 3 + conv_dim + n_heads_b * HEAD_DIM
    w_ba = jnp.pad(w_in[0][:, n_main:], ((0, 0), (0, LANES - (w_in.shape[2] - n_main))))
    cos_p, sin_p = _rope_tables(jnp.arange(s_len))
    cos_s, sin_s = _rope_tables(PAST_LEN + (jnp.arange(n_samp) % t_s))
    xp = x_prompt[0]
    xs = x_sample.reshape(n_samp, d)
    p_p, pba_p = in_proj(xp, mod_p[1], mod_p[0], g_pre_mix[0], cos_p, sin_p, w_in[0], w_ba, tm=1024, d_attn=d_attn)
    p_s, pba_s = in_proj(xs, mod_s[1], mod_s[0], g_pre_mix[0], cos_s, sin_s, w_in[0], w_ba, tm=512, d_attn=d_attn)

    n_groups = n_main // d_attn
    outs = [attn_prompt_pattern(p_p, dil, d_attn=d_attn, n_col_groups=n_groups) for _, dil in DILATED_PATTERNS]
    oa_p = [o for o, _ in outs]
    ls_p = [l for _, l in outs]
    ck = cache_win_k[0].reshape(b_s, -1, d_attn)
    cv = cache_win_v[0].reshape(b_s, -1, d_attn)
    oa_s = attn_sample(p_s, ck, cv, d_attn=d_attn, t_new=t_s)

    alog_row = _lane_row(a_log[0], n_heads_b)
    dtb_row = _lane_row(dt_bias[0], n_heads_b)
    qkv_blk = (3 * d_attn) // conv_dim
    z_blk = (3 * d_attn + conv_dim) // (n_heads_b * HEAD_DIM)
    assert qkv_blk * conv_dim == 3 * d_attn and z_blk * n_heads_b * HEAD_DIM == 3 * d_attn + conv_dim
    ob_p, tail_p, ssm_p = gdn_prompt(p_p, pba_p, conv_w[0], alog_row, dtb_row, g_gdn_norm[0],
                                     n_heads=n_heads_b, qkv_col_block=qkv_blk, z_col_block=z_blk)
    prev8 = jnp.pad(state_conv[0], ((0, 0), (SUBLANES - (CONV_W - 1), 0), (0, 0)))
    ob_s, ssm_s = gdn_sample(p_s, pba_s, prev8, state_ssm[0], conv_w[0], alog_row, dtb_row, g_gdn_norm[0],
                             n_heads=n_heads_b, t_new=t_s, qkv_col_block=qkv_blk, z_col_block=z_blk)

    wo16 = w_out[0].astype(BF16)
    wr = jnp.pad(w_router[0], ((0, 0), (0, LANES - N_EXPERTS)))
    wr_hi = wr.astype(BF16)
    wr_lo = (wr - wr_hi.astype(F32)).astype(BF16)
    br_row = jnp.full((1, LANES), NEG_BIG, F32).at[0, :N_EXPERTS].set(b_router[0])
    x1_p, h2_p, tw_p, ti_p = out_proj(oa_p, ls_p, ob_p, xp, mod_p[2], mod_p[4], mod_p[3], g_post_mix[0], g_pre_ffn[0],
                                      wo16, wr_hi, wr_lo, br_row, tm=256)
    x1_s, h2_s, tw_s, ti_s = out_proj([oa_s], [], ob_s, xs, mod_s[2], mod_s[4], mod_s[3], g_post_mix[0], g_pre_ffn[0],
                                      wo16, wr_hi, wr_lo, br_row, tm=256)

    h2 = jnp.concatenate([h2_p, h2_s], axis=0)
    top_e = jnp.concatenate([ti_p[:, :TOP_K], ti_s[:, :TOP_K]], axis=0)
    slot_tok, pos, item_e, item_ns = moe_routing(top_e)
    x_sorted = jnp.take(h2, slot_tok, axis=0)
    y_sorted = moe_experts(x_sorted, item_e, item_ns, w_gate_up[0], b_gate_up[0], w_down[0], b_down[0])
    yk = jnp.take(y_sorted, pos.reshape(-1), axis=0).reshape(s_len + n_samp, TOP_K * d)
    y_p = finish(x1_p, yk[:s_len], tw_p, mod_p[5], g_post_ffn[0], tm=256)
    y_s = finish(x1_s, yk[s_len:], tw_s, mod_s[5], g_post_ffn[0], tm=256)

    hd = HEAD_DIM
    k_p = p_p[s_len - win_p:, d_attn:2 * d_attn].reshape(1, 1, win_p, n_heads_a, hd)
    v_p = p_p[s_len - win_p:, 2 * d_attn:3 * d_attn].reshape(1, 1, win_p, n_heads_a, hd)
    conv_p = tail_p[SUBLANES - (CONV_W - 1):].reshape(1, 1, CONV_W - 1, conv_dim)
    k_s = p_s[:, d_attn:2 * d_attn].reshape(1, b_s, t_s, n_heads_a, hd)
    v_s = p_s[:, 2 * d_attn:3 * d_attn].reshape(1, b_s, t_s, n_heads_a, hd)
    conv_s = p_s.reshape(b_s, t_s, -1)[:, t_s - (CONV_W - 1):, 3 * d_attn:3 * d_attn + conv_dim][None]
    return (y_p[None], y_s.reshape(b_s, t_s, d), k_p, v_p, conv_p, ssm_p[None, None],
            k_s, v_s, conv_s, ssm_s[None])
```

```python
import functools
import math

import numpy as np
import jax
import jax.numpy as jnp
from jax import lax
from jax.experimental import pallas as pl
from jax.experimental.pallas import tpu as pltpu

F32 = jnp.float32
BF16 = jnp.bfloat16

HEAD_DIM = 128
DILATED_PATTERNS = ((128, 1), (512, 4), (2048, 16))
ATTN_BLOCK = 128
PAST_LEN = 2048
ROPE_THETA = 10000.0
CONV_W = 4
N_EXPERTS = 32
TOP_K = 4
SWIGLU_LIMIT = 7.0
SWIGLU_ALPHA = 1.702
NORM_EPS = 1e-6
L2_EPS = 1e-6

LANES = 128
SUBLANES = 8
VMEM_LIMIT_BYTES = 56 * 1024 * 1024

NEG_BIG = -1e30

GDN_CHUNK = 128
MOE_BLOCK_ROWS = 1024
MOE_SUB_ROWS = 256
MOE_FF_TILE = 256


def _cparams(*sem):
    return pltpu.CompilerParams(dimension_semantics=sem, vmem_limit_bytes=VMEM_LIMIT_BYTES)


def _sigmoid(x):
    return 1.0 / (1.0 + jnp.exp(-x))


def _softplus(x):
    return jnp.maximum(x, 0.0) + jnp.log(1.0 + jnp.exp(-jnp.abs(x)))


def _dot(a, b):
    return jnp.dot(a, b, preferred_element_type=F32)


def _dot_nt(a, b):
    return lax.dot_general(a, b, (((1,), (1,)), ((), ())), preferred_element_type=F32)


def _split3(x):
    hi = x.astype(BF16)
    r = x - hi.astype(F32)
    mid = r.astype(BF16)
    lo = (r - mid.astype(F32)).astype(BF16)
    return hi, mid, lo


def _ada_kernel(c_ref, w_ref, b_ref, o_ref):
    c = c_ref[...]
    a = (c * _sigmoid(c)).astype(BF16)
    o_ref[...] = _dot(a, w_ref[...].astype(BF16)) + b_ref[...]


def ada_mod(c, w_ada, b_ada):
    rows, d = c.shape
    n = w_ada.shape[1]
    tn = 1024
    return pl.pallas_call(
        _ada_kernel,
        grid=(n // tn,),
        in_specs=[pl.BlockSpec((rows, d), lambda j: (0, 0)),
                  pl.BlockSpec((d, tn), lambda j: (0, j)),
                  pl.BlockSpec((1, tn), lambda j: (0, j))],
        out_specs=pl.BlockSpec((rows, tn), lambda j: (0, j)),
        out_shape=jax.ShapeDtypeStruct((rows, n), F32),
        compiler_params=_cparams("arbitrary"),
        name="ada_mod",
    )(c, w_ada, b_ada.reshape(1, n))


def _inproj_kernel(x_ref, sc_ref, sh_ref, g_ref, cos_ref, sin_ref, w_ref, wba_ref, o_ref, oba_ref, h_scr,
                   *, n_rope_tiles, n_q_tiles, heads_per_tile, q_scale, norm_rows_chunk):
    j = pl.program_id(1)

    @pl.when(j == 0)
    def _():
        wba = wba_ref[...].astype(BF16)
        per_row_mod = sc_ref.shape[0] != 1

        def norm_rows(ci, carry):
            rs = pl.ds(pl.multiple_of(ci * norm_rows_chunk, norm_rows_chunk), norm_rows_chunk)
            x = x_ref[rs, :]
            ms = jnp.mean(x * x, axis=-1, keepdims=True)
            y = x * lax.rsqrt(ms + NORM_EPS) * g_ref[...]
            sc = sc_ref[rs, :] if per_row_mod else sc_ref[...]
            sh = sh_ref[rs, :] if per_row_mod else sh_ref[...]
            hb = (y * (1.0 + sc) + sh).astype(BF16)
            h_scr[rs, :] = hb
            oba_ref[rs, :] = _dot(hb, wba)
            return carry
        lax.fori_loop(0, x_ref.shape[0] // norm_rows_chunk, norm_rows, 0)

    acc = _dot(h_scr[...], w_ref[...].astype(BF16))

    @pl.when(j < n_rope_tiles)
    def _():
        scale = jnp.where(j < n_q_tiles, q_scale, 1.0).astype(F32)
        cos = cos_ref[...] * scale
        sin = sin_ref[...] * scale
        for h in range(heads_per_tile):
            xh = acc[:, h * HEAD_DIM:(h + 1) * HEAD_DIM]
            o_ref[:, h * HEAD_DIM:(h + 1) * HEAD_DIM] = xh * cos + pltpu.roll(xh, HEAD_DIM // 2, 1) * sin

    @pl.when(j >= n_rope_tiles)
    def _():
        o_ref[...] = acc


def in_proj(x, scale, shift, g, cos, sin, w_in, w_ba, *, tm, d_attn):
    rows, d = x.shape
    tm = min(tm, rows)
    tn = 512
    n_main = (w_in.shape[1] // tn) * tn
    mod_rows = scale.shape[0]
    mod_block = (tm, d) if mod_rows == rows else (1, d)
    mod_map = (lambda i, j: (i, 0)) if mod_rows == rows else (lambda i, j: (0, 0))
    kern = functools.partial(_inproj_kernel, n_rope_tiles=2 * d_attn // tn, n_q_tiles=d_attn // tn,
                             heads_per_tile=tn // HEAD_DIM, q_scale=HEAD_DIM ** -0.5, norm_rows_chunk=min(256, tm))
    return pl.pallas_call(
        kern,
        grid=(rows // tm, n_main // tn),
        in_specs=[pl.BlockSpec((tm, d), lambda i, j: (i, 0)),
                  pl.BlockSpec(mod_block, mod_map),
                  pl.BlockSpec(mod_block, mod_map),
                  pl.BlockSpec((1, d), lambda i, j: (0, 0)),
                  pl.BlockSpec((tm, HEAD_DIM), lambda i, j: (i, 0)),
                  pl.BlockSpec((tm, HEAD_DIM), lambda i, j: (i, 0)),
                  pl.BlockSpec((d, tn), lambda i, j: (0, j)),
                  pl.BlockSpec((d, LANES), lambda i, j: (0, 0))],
        out_specs=[pl.BlockSpec((tm, tn), lambda i, j: (i, j)),
                   pl.BlockSpec((tm, LANES), lambda i, j: (i, 0))],
        out_shape=[jax.ShapeDtypeStruct((rows, n_main), F32),
                   jax.ShapeDtypeStruct((rows, LANES), F32)],
        scratch_shapes=[pltpu.VMEM((tm, d), BF16)],
        compiler_params=_cparams("arbitrary", "arbitrary"),
        name="in_proj",
    )(x, scale, shift, g.reshape(1, d), cos, sin, w_in, w_ba)


ATTN_SPAN = ATTN_BLOCK * max(d for _, d in DILATED_PATTERNS)
ATTN_UNROLL = 4


def _strided_rows(start, size, stride):
    return pl.ds(start, size) if stride == 1 else pl.ds(start, size, stride=stride)


def _attn_prompt_kernel(q_ref, kp_ref, kc_ref, vp_ref, vc_ref, o_ref, kbuf, vbuf, *stat_bufs):
    n = pl.program_id(0)
    span = ATTN_SPAN
    blk = ATTN_BLOCK
    kbuf[0:span, :] = kp_ref[...]
    kbuf[span:2 * span, :] = kc_ref[...]
    vbuf[0:span, :] = vp_ref[...]
    vbuf[span:2 * span, :] = vc_ref[...]
    qi = lax.broadcasted_iota(jnp.int32, (blk, 2 * blk), 0)
    kj = lax.broadcasted_iota(jnp.int32, (blk, 2 * blk), 1)
    dist = qi + blk - kj
    bias = jnp.where(dist < 0, NEG_BIG, jnp.where(dist > blk, NEG_BIG, 0.0)).astype(F32)
    bias_first = jnp.where(kj < blk, NEG_BIG, bias)

    for pi, (_, dil) in enumerate(DILATED_PATTERNS):
        o_buf, l_buf = stat_bufs[2 * pi], stat_bufs[2 * pi + 1]
        unit_span = blk * dil

        def unit(u, carry, dil=dil, unit_span=unit_span, o_buf=o_buf, l_buf=l_buf):
            sub = u // dil
            start = sub * unit_span + (u - sub * dil)
            q = q_ref[_strided_rows(start, blk, dil), :].astype(BF16)
            kcat = kbuf[_strided_rows(span + start - unit_span, 2 * blk, dil), :].astype(BF16)
            vcat = vbuf[_strided_rows(span + start - unit_span, 2 * blk, dil), :].astype(BF16)
            first = jnp.logical_and(n == 0, sub == 0)
            s = _dot_nt(q, kcat) + jnp.where(first, bias_first, bias)
            m = jnp.max(s, axis=-1, keepdims=True)
            p = jnp.exp(s - m)
            l = jnp.sum(p, axis=-1, keepdims=True)
            o_buf[_strided_rows(start, blk, dil), :] = _dot(p.astype(BF16), vcat) / l
            l_buf[_strided_rows(start, blk, dil), :] = jnp.broadcast_to(m + jnp.log(l), (blk, HEAD_DIM))
            return carry
        lax.fori_loop(0, span // blk, unit, 0, unroll=ATTN_UNROLL)

    def merge(bi, carry):
        rs = pl.ds(pl.multiple_of(bi * blk, blk), blk)
        lses = [stat_bufs[2 * pi + 1][rs, :] for pi in range(len(DILATED_PATTERNS))]
        mx = lses[0]
        for x in lses[1:]:
            mx = jnp.maximum(mx, x)
        num = jnp.zeros((blk, HEAD_DIM), F32)
        den = jnp.zeros((blk, HEAD_DIM), F32)
        for pi, x in enumerate(lses):
            wgt = jnp.exp(x - mx)
            num = num + wgt * stat_bufs[2 * pi][rs, :]
            den = den + wgt
        o_ref[rs, :] = num / den
        return carry
    lax.fori_loop(0, span // blk, merge, 0)


def attn_prompt(p, *, n_heads):
    s = p.shape[0]
    span = ATTN_SPAN
    assert s % span == 0
    for window, dil in DILATED_PATTERNS:
        assert window // dil == ATTN_BLOCK and dil & (dil - 1) == 0
    blk = (span, HEAD_DIM)
    n_stats = 2 * len(DILATED_PATTERNS)
    return pl.pallas_call(
        _attn_prompt_kernel,
        grid=(s // span, n_heads),
        in_specs=[pl.BlockSpec(blk, lambda n, h: (n, h)),
                  pl.BlockSpec(blk, lambda n, h: (jnp.maximum(n - 1, 0), n_heads + h)),
                  pl.BlockSpec(blk, lambda n, h: (n, n_heads + h)),
                  pl.BlockSpec(blk, lambda n, h: (jnp.maximum(n - 1, 0), 2 * n_heads + h)),
                  pl.BlockSpec(blk, lambda n, h: (n, 2 * n_heads + h))],
        out_specs=pl.BlockSpec(blk, lambda n, h: (n, h)),
        out_shape=jax.ShapeDtypeStruct((s, n_heads * HEAD_DIM), F32),
        scratch_shapes=[pltpu.VMEM((2 * span, HEAD_DIM), F32), pltpu.VMEM((2 * span, HEAD_DIM), F32)]
                       + [pltpu.VMEM((span, HEAD_DIM), F32)] * n_stats,
        compiler_params=_cparams("arbitrary", "arbitrary"),
        name="attn_prompt",
    )(p, p, p, p, p)


def _attn_sample_kernel(q_ref, kn_ref, vn_ref, kf_ref, vf_ref, kc_ref, vc_ref, cntf_ref, cntc_ref, o_ref,
                        *, n_heads, t_new, half):
    mf = kf_ref.shape[1]
    near = kc_ref.shape[1] // n_heads
    cnt_f = cntf_ref[...]
    cnt_c = cntc_ref[...]
    bias_f = jnp.where(cnt_f > 0.0, 0.0, NEG_BIG)
    bias_c = jnp.where(cnt_c > 0.0, 0.0, NEG_BIG)
    ti = lax.broadcasted_iota(jnp.int32, (t_new, 1), 0)
    for h in range(n_heads):
        sl = slice(h * HEAD_DIM, (h + 1) * HEAD_DIM)
        q = q_ref[:, sl]
        kn = kn_ref[:, sl]
        vn = vn_ref[:, sl]
        q16 = q.astype(BF16)
        far_rows = pl.ds(h, half, stride=n_heads)
        k_far = kf_ref[0, :, 0, far_rows, :].reshape(mf * half, HEAD_DIM).astype(BF16)
        v_far = vf_ref[0, :, 0, far_rows, :].reshape(mf * half, HEAD_DIM).astype(BF16)
        near_rows = pl.ds(h, near, stride=n_heads)
        k_near = kc_ref[0, near_rows, :].astype(BF16)
        v_near = vc_ref[0, near_rows, :].astype(BF16)
        s_f = _dot_nt(q16, k_far) + bias_f
        s_c = _dot_nt(q16, k_near) + bias_c
        m = jnp.maximum(jnp.max(s_f, axis=-1, keepdims=True), jnp.max(s_c, axis=-1, keepdims=True))
        s_n, c_n = [], []
        for t2 in range(t_new):
            diff = ti - t2
            c = jnp.zeros((t_new, 1), F32)
            for window, dil in DILATED_PATTERNS:
                ok = (diff >= 0) & ((diff & (dil - 1)) == 0) & (diff <= window)
                c = c + jnp.where(ok, 1.0, 0.0)
            sn = jnp.where(c > 0.0, jnp.sum(q * kn[t2:t2 + 1, :], axis=-1, keepdims=True), NEG_BIG)
            m = jnp.maximum(m, sn)
            s_n.append(sn)
            c_n.append(c)
        e_f = cnt_f * jnp.exp(s_f - m)
        e_c = cnt_c * jnp.exp(s_c - m)
        l = jnp.sum(e_f, axis=-1, keepdims=True) + jnp.sum(e_c, axis=-1, keepdims=True)
        acc = _dot(e_f.astype(BF16), v_far) + _dot(e_c.astype(BF16), v_near)
        for t2 in range(t_new):
            e_n = c_n[t2] * jnp.exp(s_n[t2] - m)
            l = l + e_n
            acc = acc + e_n * vn[t2:t2 + 1, :]
        o_ref[:, sl] = acc / l


def _sample_key_counts(t_new, cache_len):
    t = np.arange(t_new)[:, None]
    p = np.arange(cache_len)[None, :]
    diff = cache_len + t - p
    cnt = np.zeros((t_new, cache_len), np.float32)
    for window, dil in DILATED_PATTERNS:
        cnt += ((diff % dil == 0) & (diff <= window)).astype(np.float32)
    return cnt


def attn_sample(p, cache_k, cache_v, *, t_new):
    bt = p.shape[0]
    b = bt // t_new
    _, cache_len, n_heads, hd = cache_k.shape
    d_attn = n_heads * hd
    d_max = max(d for _, d in DILATED_PATTERNS)
    half = d_max // 2
    near = max(w for w, d in DILATED_PATTERNS if d != d_max)
    assert hd == HEAD_DIM and cache_len >= max(w for w, _ in DILATED_PATTERNS)
    assert t_new <= half and cache_len % d_max == 0 and near % d_max == 0 and cache_len % near == 0
    for _, dil in DILATED_PATTERNS:
        assert dil & (dil - 1) == 0
    mf = (cache_len - near) // d_max
    cnt = _sample_key_counts(t_new, cache_len)
    far_pos = (np.arange(mf)[:, None] * d_max + np.arange(half)[None, :]).reshape(-1)
    covered = np.zeros(cache_len, bool)
    covered[far_pos] = True
    covered[cache_len - near:] = True
    assert not cnt[:, ~covered].any()
    cnt_far = jnp.asarray(cnt[:, far_pos])
    cnt_near = jnp.asarray(cnt[:, cache_len - near:])
    far_shape = (b, cache_len // d_max, 2, half * n_heads, hd)
    near_shape = (b, cache_len * n_heads, hd)
    kern = functools.partial(_attn_sample_kernel, n_heads=n_heads, t_new=t_new, half=half)
    blk = (t_new, d_attn)
    far_blk = pl.BlockSpec((1, mf, 1, half * n_heads, hd), lambda i: (i, 0, 0, 0, 0))
    near_blk = pl.BlockSpec((1, near * n_heads, hd), lambda i: (i, cache_len // near - 1, 0))
    return pl.pallas_call(
        kern,
        grid=(b,),
        in_specs=[pl.BlockSpec(blk, lambda i: (i, 0)),
                  pl.BlockSpec(blk, lambda i: (i, 1)),
                  pl.BlockSpec(blk, lambda i: (i, 2)),
                  far_blk, far_blk, near_blk, near_blk,
                  pl.BlockSpec(cnt_far.shape, lambda i: (0, 0)),
                  pl.BlockSpec(cnt_near.shape, lambda i: (0, 0))],
        out_specs=pl.BlockSpec(blk, lambda i: (i, 0)),
        out_shape=jax.ShapeDtypeStruct((bt, d_attn), F32),
        compiler_params=_cparams("arbitrary"),
        name="attn_sample",
    )(p, p, p, cache_k.reshape(far_shape), cache_v.reshape(far_shape),
      cache_k.reshape(near_shape), cache_v.reshape(near_shape), cnt_far, cnt_near)


def _causal_conv_silu(x, prev8, cw):
    t = x.shape[0]
    row8 = lax.broadcasted_iota(jnp.int32, (SUBLANES, 1), 0)
    conv = x * cw[CONV_W - 1:CONV_W, :]
    for i in range(1, CONV_W):
        xr = pltpu.roll(x, i, 0)
        pr = pltpu.roll(prev8, i, 0)
        head = jnp.where(row8 < i, pr, xr[:SUBLANES, :])
        xs = head if t == SUBLANES else jnp.concatenate([head, xr[SUBLANES:, :]], axis=0)
        conv = conv + xs * cw[CONV_W - 1 - i:CONV_W - i, :]
    return conv * _sigmoid(conv)


def _l2norm(x):
    return x * lax.rsqrt(jnp.sum(x * x, axis=-1, keepdims=True) + L2_EPS)


def _gated_rmsnorm(o, g_row, z):
    y = o * lax.rsqrt(jnp.mean(o * o, axis=-1, keepdims=True) + NORM_EPS) * g_row
    return y * (z * _sigmoid(z))


def _beta_and_g(ba, alog_row, dtb_row):
    beta = _sigmoid(ba)
    g = -jnp.exp(alog_row) * _softplus(ba + dtb_row)
    return beta, g


def _gdn_prompt_kernel(qkv_ref, z_ref, ba_ref, cw_ref, alog_ref, dtb_ref, gn_ref,
                       o_ref, tail_ref, ssm_ref, s_scr, prev_scr, *, n_heads):
    n = pl.program_id(0)
    c = GDN_CHUNK
    hd = HEAD_DIM

    @pl.when(n == 0)
    def _():
        s_scr[...] = jnp.zeros_like(s_scr)
        prev_scr[...] = jnp.zeros_like(prev_scr)

    x = qkv_ref[...]
    conv = _causal_conv_silu(x, prev_scr[...], cw_ref[...])
    prev_scr[...] = x[c - SUBLANES:, :]
    tail_ref[...] = x[c - SUBLANES:, :]

    beta_all, g_all = _beta_and_g(ba_ref[...], alog_ref[...], dtb_ref[...])
    ri = lax.broadcasted_iota(jnp.int32, (c, c), 0)
    ci = lax.broadcasted_iota(jnp.int32, (c, c), 1)
    tri_incl = ri >= ci
    tri_strict = ri > ci
    eye = jnp.where(ri == ci, 1.0, 0.0).astype(F32)
    blk_xor = ri ^ ci
    tri_b = jnp.where(tri_incl, 1.0, 0.0).astype(BF16)
    g_hi, g_mid, g_lo = _split3(g_all)
    gcum = _dot(tri_b, g_hi) + _dot(tri_b, g_mid) + _dot(tri_b, g_lo)
    gcum_t = gcum.T
    gn = gn_ref[...]

    for h in range(n_heads):
        q = _l2norm(conv[:, h * hd:(h + 1) * hd]) * (hd ** -0.5)
        k = _l2norm(conv[:, (n_heads + h) * hd:(n_heads + h + 1) * hd])
        v = conv[:, (2 * n_heads + h) * hd:(2 * n_heads + h + 1) * hd]
        beta = beta_all[:, h:h + 1]
        gc = gcum[:, n_heads + h:n_heads + h + 1]
        gr = gcum_t[n_heads + h:n_heads + h + 1, :]
        g_last = gcum[c - 1:c, n_heads + h:n_heads + h + 1]
        decay = jnp.where(tri_incl, jnp.exp(gc - gr), 0.0)
        kb = k * beta
        k16 = k.astype(BF16)
        a = jnp.where(tri_strict, _dot_nt(kb.astype(BF16), k16) * decay, 0.0)
        t_inv = eye - jnp.where((blk_xor >> 1) == 0, a, 0.0)
        for lvl in range(2, int(math.log2(c)) + 1):
            l16 = jnp.where((blk_xor >> (lvl - 1)) == 1, a, 0.0).astype(BF16)
            t16 = t_inv.astype(BF16)
            t_inv = t_inv - _dot(t16, _dot(l16, t16).astype(BF16))
        rhs = jnp.concatenate([v * beta, kb * jnp.exp(gc)], axis=-1)
        sol = _dot(t_inv.astype(BF16), rhs.astype(BF16))
        u = sol[:, :hd]
        w = sol[:, hd:]
        qk = _dot_nt(q.astype(BF16), k16) * decay

        s_old = s_scr[h]
        s16 = s_old.astype(BF16)
        v_new = u - _dot(w.astype(BF16), s16)
        v16 = v_new.astype(BF16)
        o = _dot((q * jnp.exp(gc)).astype(BF16), s16) + _dot(qk.astype(BF16), v16)
        k_dec = (k * jnp.exp(g_last - gc)).T
        s_scr[h] = s_old * jnp.exp(g_last) + _dot(k_dec.astype(BF16), v16)
        o_ref[:, h * hd:(h + 1) * hd] = _gated_rmsnorm(o, gn, z_ref[:, h * hd:(h + 1) * hd])

    @pl.when(n == pl.num_programs(0) - 1)
    def _():
        ssm_ref[...] = s_scr[...]


def gdn_prompt(p, pba, conv_w, alog_row, dtb_row, g_norm, *, n_heads, qkv_col_block, z_col_block):
    s = p.shape[0]
    c = GDN_CHUNK
    hd = HEAD_DIM
    conv_dim = 3 * n_heads * hd
    kern = functools.partial(_gdn_prompt_kernel, n_heads=n_heads)
    return pl.pallas_call(
        kern,
        grid=(s // c,),
        in_specs=[pl.BlockSpec((c, conv_dim), lambda n: (n, qkv_col_block)),
                  pl.BlockSpec((c, n_heads * hd), lambda n: (n, z_col_block)),
                  pl.BlockSpec((c, LANES), lambda n: (n, 0)),
                  pl.BlockSpec((CONV_W, conv_dim), lambda n: (0, 0)),
                  pl.BlockSpec((1, LANES), lambda n: (0, 0)),
                  pl.BlockSpec((1, LANES), lambda n: (0, 0)),
                  pl.BlockSpec((1, hd), lambda n: (0, 0))],
        out_specs=[pl.BlockSpec((c, n_heads * hd), lambda n: (n, 0)),
                   pl.BlockSpec((SUBLANES, conv_dim), lambda n: (0, 0)),
                   pl.BlockSpec((n_heads, hd, hd), lambda n: (0, 0, 0))],
        out_shape=[jax.ShapeDtypeStruct((s, n_heads * hd), F32),
                   jax.ShapeDtypeStruct((SUBLANES, conv_dim), F32),
                   jax.ShapeDtypeStruct((n_heads, hd, hd), F32)],
        scratch_shapes=[pltpu.VMEM((n_heads, hd, hd), F32), pltpu.VMEM((SUBLANES, conv_dim), F32)],
        compiler_params=_cparams("arbitrary"),
        name="gdn_prompt",
    )(p, p, pba, conv_w, alog_row, dtb_row, g_norm.reshape(1, hd))


def _gdn_sample_kernel(qkv_ref, z_ref, ba_ref, prev_ref, s0_ref, cw_ref, alog_ref, dtb_ref, gn_ref,
                       o_ref, ssm_ref, *, n_heads, t_new):
    hd = HEAD_DIM
    x = qkv_ref[...]
    conv = _causal_conv_silu(x, prev_ref[0], cw_ref[...])
    beta_all, g_all = _beta_and_g(ba_ref[...], alog_ref[...], dtb_ref[...])
    eg_all = jnp.exp(g_all)
    gn = gn_ref[...]
    pad = jnp.zeros((hd - 2 * t_new, hd), F32)
    for h in range(n_heads):
        q = _l2norm(conv[:, h * hd:(h + 1) * hd]) * (hd ** -0.5)
        k = _l2norm(conv[:, (n_heads + h) * hd:(n_heads + h + 1) * hd])
        v = conv[:, (2 * n_heads + h) * hd:(2 * n_heads + h + 1) * hd]
        kq_t = jnp.concatenate([k, q, pad], axis=0).T
        s = s0_ref[0, h]
        rows = []
        for t in range(t_new):
            kb = jnp.broadcast_to(kq_t[:, t:t + 1], (hd, hd))
            qb = jnp.broadcast_to(kq_t[:, t_new + t:t_new + t + 1], (hd, hd))
            eg = eg_all[t:t + 1, n_heads + h:n_heads + h + 1]
            bt = beta_all[t:t + 1, h:h + 1]
            s = s * eg
            ks = jnp.sum(s * kb, axis=0, keepdims=True)
            v_new = bt * (v[t:t + 1, :] - ks)
            s = s + kb * v_new
            rows.append(jnp.sum(s * qb, axis=0, keepdims=True))
        ssm_ref[0, h] = s
        o = jnp.concatenate(rows, axis=0)
        o_ref[:, h * hd:(h + 1) * hd] = _gated_rmsnorm(o, gn, z_ref[:, h * hd:(h + 1) * hd])


def gdn_sample(p, pba, prev8, s0, conv_w, alog_row, dtb_row, g_norm, *, n_heads, t_new, qkv_col_block, z_col_block):
    bt = p.shape[0]
    b = bt // t_new
    hd = HEAD_DIM
    conv_dim = 3 * n_heads * hd
    kern = functools.partial(_gdn_sample_kernel, n_heads=n_heads, t_new=t_new)
    return pl.pallas_call(
        kern,
        grid=(b,),
        in_specs=[pl.BlockSpec((t_new, conv_dim), lambda i: (i, qkv_col_block)),
                  pl.BlockSpec((t_new, n_heads * hd), lambda i: (i, z_col_block)),
                  pl.BlockSpec((t_new, LANES), lambda i: (i, 0)),
                  pl.BlockSpec((1, SUBLANES, conv_dim), lambda i: (i, 0, 0)),
                  pl.BlockSpec((1, n_heads, hd, hd), lambda i: (i, 0, 0, 0)),
                  pl.BlockSpec((CONV_W, conv_dim), lambda i: (0, 0)),
                  pl.BlockSpec((1, LANES), lambda i: (0, 0)),
                  pl.BlockSpec((1, LANES), lambda i: (0, 0)),
                  pl.BlockSpec((1, hd), lambda i: (0, 0))],
        out_specs=[pl.BlockSpec((t_new, n_heads * hd), lambda i: (i, 0)),
                   pl.BlockSpec((1, n_heads, hd, hd), lambda i: (i, 0, 0, 0))],
        out_shape=[jax.ShapeDtypeStruct((bt, n_heads * hd), F32),
                   jax.ShapeDtypeStruct((b, n_heads, hd, hd), F32)],
        compiler_params=_cparams("arbitrary"),
        name="gdn_sample",
    )(p, p, pba, prev8, s0, conv_w, alog_row, dtb_row, g_norm.reshape(1, hd))


def _outproj_kernel(oa_ref, ob_ref, x_ref, gate_ref, sc_ref, sh_ref, gpost_ref, gpre_ref, wo_ref, wr_hi_ref, wr_lo_ref,
                    br_ref, x1_ref, h2_ref, tw_ref, ti_ref):
    d_attn = oa_ref.shape[1]
    mix = (_dot(oa_ref[...].astype(BF16), wo_ref[:d_attn, :]) + _dot(ob_ref[...].astype(BF16), wo_ref[d_attn:, :]))
    y = mix * lax.rsqrt(jnp.mean(mix * mix, axis=-1, keepdims=True) + NORM_EPS) * gpost_ref[...]
    x1 = x_ref[...] + gate_ref[...] * y
    x1_ref[...] = x1
    hn = x1 * lax.rsqrt(jnp.mean(x1 * x1, axis=-1, keepdims=True) + NORM_EPS) * gpre_ref[...]
    h2 = hn * (1.0 + sc_ref[...]) + sh_ref[...]
    h2_ref[...] = h2.astype(BF16)

    h_hi = h2.astype(BF16)
    h_lo = (h2 - h_hi.astype(F32)).astype(BF16)
    logits = (_dot(h_hi, wr_hi_ref[...]) + _dot(h_hi, wr_lo_ref[...]) + _dot(h_lo, wr_hi_ref[...])) + br_ref[...]
    lane = lax.broadcasted_iota(jnp.int32, logits.shape, 1)
    tops, idxs = [], []
    for _ in range(TOP_K):
        mx = jnp.max(logits, axis=-1, keepdims=True)
        ix = jnp.min(jnp.where(logits == mx, lane, LANES), axis=-1, keepdims=True)
        tops.append(mx)
        idxs.append(ix)
        logits = jnp.where(lane == ix, 2.0 * NEG_BIG, logits)
    es = [jnp.exp(t - tops[0]) for t in tops]
    tot = es[0]
    for e in es[1:]:
        tot = tot + e
    tw = jnp.zeros(logits.shape, F32)
    ti = jnp.zeros(logits.shape, jnp.int32)
    for kk in range(TOP_K):
        tw = jnp.where(lane == kk, es[kk] / tot, tw)
        ti = jnp.where(lane == kk, idxs[kk], ti)
    tw_ref[...] = tw
    ti_ref[...] = ti


def out_proj(o_a, o_b, x, gate, scale, shift, g_post, g_pre, wo16, wr_hi, wr_lo, br_row, *, tm):
    rows, d = x.shape
    tm = min(tm, rows)
    d_attn = o_a.shape[1]
    mod_rows = gate.shape[0]
    mod_block = (tm, d) if mod_rows == rows else (1, d)
    mod_map = (lambda i: (i, 0)) if mod_rows == rows else (lambda i: (0, 0))
    half = pl.BlockSpec((tm, d_attn), lambda i: (i, 0))
    full = pl.BlockSpec((tm, d), lambda i: (i, 0))
    row_d = pl.BlockSpec((1, d), lambda i: (0, 0))
    lane_blk = pl.BlockSpec((tm, LANES), lambda i: (i, 0))
    in_specs = [half, pl.BlockSpec((tm, d - d_attn), lambda i: (i, 0)), full,
                pl.BlockSpec(mod_block, mod_map), pl.BlockSpec(mod_block, mod_map), pl.BlockSpec(mod_block, mod_map),
                row_d, row_d,
                pl.BlockSpec((d, d), lambda i: (0, 0)),
                pl.BlockSpec((d, LANES), lambda i: (0, 0)), pl.BlockSpec((d, LANES), lambda i: (0, 0)),
                pl.BlockSpec((1, LANES), lambda i: (0, 0))]
    args = [o_a, o_b, x, gate, scale, shift, g_post.reshape(1, d), g_pre.reshape(1, d), wo16, wr_hi, wr_lo, br_row]
    return pl.pallas_call(
        _outproj_kernel,
        grid=(rows // tm,),
        in_specs=in_specs,
        out_specs=[full, full, lane_blk, lane_blk],
        out_shape=[jax.ShapeDtypeStruct((rows, d), F32), jax.ShapeDtypeStruct((rows, d), BF16),
                   jax.ShapeDtypeStruct((rows, LANES), F32), jax.ShapeDtypeStruct((rows, LANES), jnp.int32)],
        compiler_params=_cparams("arbitrary"),
        name="out_proj",
    )(*args)


def _moe_kernel(e_ref, ns_ref, x_ref, wg_ref, wu_ref, bg_ref, bu_ref, wd_ref, bd_ref, o_ref):
    w = pl.program_id(0)
    j = pl.program_id(1)
    ns = ns_ref[w]
    sb = MOE_SUB_ROWS
    n_sub = MOE_BLOCK_ROWS // sb

    @pl.when(j == 0)
    def _():
        def zero(s, carry):
            r0 = pl.multiple_of(s * sb, sb)
            o_ref[pl.ds(r0, sb), :] = jnp.zeros((sb, o_ref.shape[1]), F32)
            return carry
        lax.fori_loop(ns, n_sub, zero, 0)

    @pl.when(ns > 0)
    def _():
        wg = wg_ref[0].astype(BF16)
        wu = wu_ref[0].astype(BF16)
        wd = wd_ref[0].astype(BF16)
        bg = bg_ref[0]
        bu = bu_ref[0]
        bd = bd_ref[0]

        def body(s, carry):
            r0 = pl.multiple_of(s * sb, sb)
            xs = x_ref[pl.ds(r0, sb), :]
            gate = jnp.minimum(_dot(xs, wg) + bg, SWIGLU_LIMIT)
            up = jnp.clip(_dot(xs, wu) + bu, -SWIGLU_LIMIT, SWIGLU_LIMIT)
            act = gate * _sigmoid(SWIGLU_ALPHA * gate) * (up + 1.0)
            y = _dot(act.astype(BF16), wd)

            @pl.when(j == 0)
            def _():
                o_ref[pl.ds(r0, sb), :] = y + bd

            @pl.when(j > 0)
            def _():
                o_ref[pl.ds(r0, sb), :] += y
            return carry
        lax.fori_loop(0, ns, body, 0)


def moe_experts(x_sorted, item_e, item_ns, w_gate_up, b_gate_up, w_down, b_down):
    n_rows, d = x_sorted.shape
    n_e, _, two_ff = w_gate_up.shape
    d_ff = two_ff // 2
    bm = MOE_BLOCK_ROWS
    tf = MOE_FF_TILE
    n_f = d_ff // tf
    n_items = n_rows // bm

    def jj(j, ns, w):
        return jnp.where(ns[w] > 0, j, n_f - 1)

    grid_spec = pltpu.PrefetchScalarGridSpec(
        num_scalar_prefetch=2,
        grid=(n_items, n_f),
        in_specs=[pl.BlockSpec((bm, d), lambda w, j, e, ns: (w, 0)),
                  pl.BlockSpec((1, d, tf), lambda w, j, e, ns: (e[w], 0, jj(j, ns, w))),
                  pl.BlockSpec((1, d, tf), lambda w, j, e, ns: (e[w], 0, n_f + jj(j, ns, w))),
                  pl.BlockSpec((1, 1, tf), lambda w, j, e, ns: (e[w], 0, jj(j, ns, w))),
                  pl.BlockSpec((1, 1, tf), lambda w, j, e, ns: (e[w], 0, n_f + jj(j, ns, w))),
                  pl.BlockSpec((1, tf, d), lambda w, j, e, ns: (e[w], jj(j, ns, w), 0)),
                  pl.BlockSpec((1, 1, d), lambda w, j, e, ns: (e[w], 0, 0))],
        out_specs=pl.BlockSpec((bm, d), lambda w, j, e, ns: (w, 0)),
    )
    return pl.pallas_call(
        _moe_kernel,
        grid_spec=grid_spec,
        out_shape=jax.ShapeDtypeStruct((n_rows, d), F32),
        compiler_params=_cparams("arbitrary", "arbitrary"),
        name="moe_experts",
    )(item_e, item_ns, x_sorted, w_gate_up, w_gate_up, b_gate_up.reshape(n_e, 1, two_ff),
      b_gate_up.reshape(n_e, 1, two_ff), w_down, b_down.reshape(n_e, 1, d))


def moe_routing(top_e):
    n_tok = top_e.shape[0]
    n_asg = n_tok * TOP_K
    bm = MOE_BLOCK_ROWS
    n_items = n_asg // bm + N_EXPERTS
    e_flat = top_e.reshape(-1)
    order = jnp.argsort(e_flat).astype(jnp.int32)
    e_sorted = e_flat[order]
    counts = jnp.bincount(e_flat, length=N_EXPERTS).astype(jnp.int32)
    padded = (counts + bm - 1) // bm * bm
    pad_end = jnp.cumsum(padded)
    seg_start = pad_end - padded
    cnt_start = jnp.cumsum(counts) - counts
    slot = seg_start[e_sorted] + jnp.arange(n_asg, dtype=jnp.int32) - cnt_start[e_sorted]
    slot_tok = jnp.zeros((n_items * bm,), jnp.int32).at[slot].set(order // TOP_K)
    pos = jnp.zeros((n_asg,), jnp.int32).at[order].set(slot).reshape(n_tok, TOP_K)
    item_start = jnp.arange(n_items, dtype=jnp.int32) * bm
    item_e_raw = jnp.minimum(jnp.sum(pad_end[None, :] <= item_start[:, None], axis=1), N_EXPERTS - 1).astype(jnp.int32)
    rows_left = jnp.where(item_start < pad_end[-1], counts[item_e_raw] - (item_start - seg_start[item_e_raw]), 0)
    item_ns = (jnp.clip(rows_left, 0, bm) + MOE_SUB_ROWS - 1) // MOE_SUB_ROWS
    last_e = item_e_raw[jnp.maximum(pad_end[-1] // bm - 1, 0)]
    item_e = jnp.where(item_start < pad_end[-1], item_e_raw, last_e)
    return slot_tok, pos, item_e.astype(jnp.int32), item_ns.astype(jnp.int32)


FINISH_ISSUE_UNROLL = 8


def _finish_kernel(pos_ref, x1_ref, tw_ref, gate_ref, g_ref, y_hbm, o_ref, ybuf, sem):
    i = pl.program_id(0)
    n_tiles = pl.num_programs(0)
    tm, d = x1_ref.shape

    def row_copy(tile, slot, idx):
        kk = idx // tm
        t = idx - kk * tm
        src = pos_ref[(tile * tm + t) * TOP_K + kk]
        return pltpu.make_async_copy(y_hbm.at[pl.ds(src, 1), :], ybuf.at[slot, kk, pl.ds(t, 1), :], sem.at[slot])

    def issue(tile, slot):
        def body(idx, carry):
            row_copy(tile, slot, idx).start()
            return carry
        lax.fori_loop(0, TOP_K * tm, body, 0, unroll=FINISH_ISSUE_UNROLL)

    def wait(tile, slot):
        def body(idx, carry):
            row_copy(tile, slot, idx).wait()
            return carry
        lax.fori_loop(0, TOP_K * tm, body, 0, unroll=FINISH_ISSUE_UNROLL)

    @pl.when(i == 0)
    def _():
        issue(0, 0)

    @pl.when(i + 1 < n_tiles)
    def _():
        issue(i + 1, (i + 1) % 2)

    slot = i % 2
    wait(i, slot)
    tw = tw_ref[...]
    ff = tw[:, 0:1] * ybuf[slot, 0]
    for kk in range(1, TOP_K):
        ff = ff + tw[:, kk:kk + 1] * ybuf[slot, kk]
    y = ff * lax.rsqrt(jnp.mean(ff * ff, axis=-1, keepdims=True) + NORM_EPS) * g_ref[...]
    o_ref[...] = x1_ref[...] + gate_ref[...] * y


def finish(x1, y_sorted, pos, tw, gate, g_post, *, tm):
    rows, d = x1.shape
    tm = min(tm, rows)
    mod_rows = gate.shape[0]
    mod_block = (tm, d) if mod_rows == rows else (1, d)
    mod_map = (lambda i, pos: (i, 0)) if mod_rows == rows else (lambda i, pos: (0, 0))
    grid_spec = pltpu.PrefetchScalarGridSpec(
        num_scalar_prefetch=1,
        grid=(rows // tm,),
        in_specs=[pl.BlockSpec((tm, d), lambda i, pos: (i, 0)),
                  pl.BlockSpec((tm, LANES), lambda i, pos: (i, 0)),
                  pl.BlockSpec(mod_block, mod_map),
                  pl.BlockSpec((1, d), lambda i, pos: (0, 0)),
                  pl.BlockSpec(memory_space=pl.ANY)],
        out_specs=pl.BlockSpec((tm, d), lambda i, pos: (i, 0)),
        scratch_shapes=[pltpu.VMEM((2, TOP_K, tm, d), F32), pltpu.SemaphoreType.DMA((2,))],
    )
    return pl.pallas_call(
        _finish_kernel,
        grid_spec=grid_spec,
        out_shape=jax.ShapeDtypeStruct((rows, d), F32),
        compiler_params=_cparams("arbitrary"),
        name="finish",
    )(pos, x1, tw, gate, g_post.reshape(1, d), y_sorted)


def _rope_tables(pos):
    half = HEAD_DIM // 2
    inv_freq = ROPE_THETA ** (-jnp.arange(half, dtype=F32) / half)
    ang = pos.astype(F32)[:, None] * inv_freq[None, :]
    cos = jnp.cos(ang)
    sin = jnp.sin(ang)
    return jnp.concatenate([cos, cos], axis=-1), jnp.concatenate([-sin, sin], axis=-1)


def _lane_row(vals, offset):
    return jnp.zeros((1, LANES), F32).at[0, offset:offset + vals.shape[0]].set(vals.astype(F32))


def kernel(x_prompt, x_sample, c_prompt, c_sample, cache_win_k, cache_win_v, state_conv, state_ssm, w_ada, b_ada, g_pre_mix, g_post_mix, g_pre_ffn, g_post_ffn, w_in, conv_w, a_log, dt_bias, g_gdn_norm, w_out, w_router, b_router, w_gate_up, b_gate_up, w_down, b_down):
    depth = w_ada.shape[0]
    assert depth == 1 and x_prompt.shape[0] == 1
    _, s_len, d = x_prompt.shape
    b_s, t_s, _ = x_sample.shape
    n_heads_a = cache_win_k.shape[3]
    d_attn = n_heads_a * HEAD_DIM
    n_heads_b = state_ssm.shape[2]
    conv_dim = 3 * n_heads_b * HEAD_DIM
    n_samp = b_s * t_s
    win_p = min(max(w for w, _ in DILATED_PATTERNS), s_len)

    c_all = jnp.concatenate([c_prompt, c_sample], axis=0)
    c_rows = -(-c_all.shape[0] // SUBLANES) * SUBLANES
    c_all = jnp.pad(c_all, ((0, c_rows - c_all.shape[0]), (0, 0)))
    mod = ada_mod(c_all, w_ada[0], b_ada[0])
    mod_p = [mod[0:1, i * d:(i + 1) * d] for i in range(6)]
    mod_s = [jnp.repeat(mod[1:1 + b_s, i * d:(i + 1) * d], t_s, axis=0) for i in range(6)]

    n_main = d_attn * 3 + conv_dim + n_heads_b * HEAD_DIM
    w_ba = jnp.pad(w_in[0][:, n_main:], ((0, 0), (0, LANES - (w_in.shape[2] - n_main))))
    cos_p, sin_p = _rope_tables(jnp.arange(s_len))
    cos_s, sin_s = _rope_tables(PAST_LEN + (jnp.arange(n_samp) % t_s))
    xp = x_prompt[0]
    xs = x_sample.reshape(n_samp, d)
    p_p, pba_p = in_proj(xp, mod_p[1], mod_p[0], g_pre_mix[0], cos_p, sin_p, w_in[0], w_ba, tm=1024, d_attn=d_attn)
    p_s, pba_s = in_proj(xs, mod_s[1], mod_s[0], g_pre_mix[0], cos_s, sin_s, w_in[0], w_ba, tm=512, d_attn=d_attn)

    oa_p = attn_prompt(p_p, n_heads=n_heads_a)
    oa_s = attn_sample(p_s, cache_win_k[0], cache_win_v[0], t_new=t_s)

    alog_row = _lane_row(a_log[0], n_heads_b)
    dtb_row = _lane_row(dt_bias[0], n_heads_b)
    qkv_blk = (3 * d_attn) // conv_dim
    z_blk = (3 * d_attn + conv_dim) // (n_heads_b * HEAD_DIM)
    assert qkv_blk * conv_dim == 3 * d_attn and z_blk * n_heads_b * HEAD_DIM == 3 * d_attn + conv_dim
    ob_p, tail_p, ssm_p = gdn_prompt(p_p, pba_p, conv_w[0], alog_row, dtb_row, g_gdn_norm[0],
                                     n_heads=n_heads_b, qkv_col_block=qkv_blk, z_col_block=z_blk)
    prev8 = jnp.pad(state_conv[0], ((0, 0), (SUBLANES - (CONV_W - 1), 0), (0, 0)))
    ob_s, ssm_s = gdn_sample(p_s, pba_s, prev8, state_ssm[0], conv_w[0], alog_row, dtb_row, g_gdn_norm[0],
                             n_heads=n_heads_b, t_new=t_s, qkv_col_block=qkv_blk, z_col_block=z_blk)

    wo16 = w_out[0].astype(BF16)
    wr = jnp.pad(w_router[0], ((0, 0), (0, LANES - N_EXPERTS)))
    wr_hi = wr.astype(BF16)
    wr_lo = (wr - wr_hi.astype(F32)).astype(BF16)
    br_row = jnp.full((1, LANES), NEG_BIG, F32).at[0, :N_EXPERTS].set(b_router[0])
    x1_p, h2_p, tw_p, ti_p = out_proj(oa_p, ob_p, xp, mod_p[2], mod_p[4], mod_p[3], g_post_mix[0], g_pre_ffn[0],
                                      wo16, wr_hi, wr_lo, br_row, tm=256)
    x1_s, h2_s, tw_s, ti_s = out_proj(oa_s, ob_s, xs, mod_s[2], mod_s[4], mod_s[3], g_post_mix[0], g_pre_ffn[0],
                                      wo16, wr_hi, wr_lo, br_row, tm=256)

    h2 = jnp.concatenate([h2_p, h2_s], axis=0)
    top_e = jnp.concatenate([ti_p[:, :TOP_K], ti_s[:, :TOP_K]], axis=0)
    slot_tok, pos, item_e, item_ns = moe_routing(top_e)
    x_sorted = jnp.take(h2, slot_tok, axis=0)
    y_sorted = moe_experts(x_sorted, item_e, item_ns, w_gate_up[0], b_gate_up[0], w_down[0], b_down[0])
    y_p = finish(x1_p, y_sorted, pos[:s_len].reshape(-1), tw_p, mod_p[5], g_post_ffn[0], tm=128)
    y_s = finish(x1_s, y_sorted, pos[s_len:].reshape(-1), tw_s, mod_s[5], g_post_ffn[0], tm=128)

    hd = HEAD_DIM
    k_p = p_p[s_len - win_p:, d_attn:2 * d_attn].reshape(1, 1, win_p, n_heads_a, hd)
    v_p = p_p[s_len - win_p:, 2 * d_attn:3 * d_attn].reshape(1, 1, win_p, n_heads_a, hd)
    conv_p = tail_p[SUBLANES - (CONV_W - 1):].reshape(1, 1, CONV_W - 1, conv_dim)
    k_s = p_s[:, d_attn:2 * d_attn].reshape(1, b_s, t_s, n_heads_a, hd)
    v_s = p_s[:, 2 * d_attn:3 * d_attn].reshape(1, b_s, t_s, n_heads_a, hd)
    conv_s = p_s.reshape(b_s, t_s, -1)[:, t_s - (CONV_W - 1):, 3 * d_attn:3 * d_attn + conv_dim][None]
    return (y_p[None], y_s.reshape(b_s, t_s, d), k_p, v_p, conv_p, ssm_p[None, None],
            k_s, v_s, conv_s, ssm_s[None])
```

```python
import functools
import math

import numpy as np
import jax
import jax.numpy as jnp
from jax import lax
from jax.experimental import pallas as pl
from jax.experimental.pallas import tpu as pltpu

F32 = jnp.float32
BF16 = jnp.bfloat16

HEAD_DIM = 128
DILATED_PATTERNS = ((128, 1), (512, 4), (2048, 16))
ATTN_BLOCK = 128
PAST_LEN = 2048
ROPE_THETA = 10000.0
CONV_W = 4
N_EXPERTS = 32
TOP_K = 4
SWIGLU_LIMIT = 7.0
SWIGLU_ALPHA = 1.702
NORM_EPS = 1e-6
L2_EPS = 1e-6

LANES = 128
SUBLANES = 8
VMEM_LIMIT_BYTES = 56 * 1024 * 1024

NEG_BIG = -1e30

GDN_CHUNK = 128
MOE_BLOCK_ROWS = 1024
MOE_SUB_ROWS = 256
MOE_FF_TILE = 256
MOE_DOWN_CHUNK = 512


def _cparams(*sem):
    return pltpu.CompilerParams(dimension_semantics=sem, vmem_limit_bytes=VMEM_LIMIT_BYTES)


def _sigmoid(x):
    return 1.0 / (1.0 + jnp.exp(-x))


def _softplus(x):
    return jnp.maximum(x, 0.0) + jnp.log(1.0 + jnp.exp(-jnp.abs(x)))


def _dot(a, b):
    return jnp.dot(a, b, preferred_element_type=F32)


def _dot_nt(a, b):
    return lax.dot_general(a, b, (((1,), (1,)), ((), ())), preferred_element_type=F32)


def _split3(x):
    hi = x.astype(BF16)
    r = x - hi.astype(F32)
    mid = r.astype(BF16)
    lo = (r - mid.astype(F32)).astype(BF16)
    return hi, mid, lo


def _ada_kernel(c_ref, w_ref, b_ref, o_ref):
    c = c_ref[...]
    a = (c * _sigmoid(c)).astype(BF16)
    o_ref[...] = _dot(a, w_ref[...].astype(BF16)) + b_ref[...]


def ada_mod(c, w_ada, b_ada):
    rows, d = c.shape
    n = w_ada.shape[1]
    tn = 1024
    return pl.pallas_call(
        _ada_kernel,
        grid=(n // tn,),
        in_specs=[pl.BlockSpec((rows, d), lambda j: (0, 0)),
                  pl.BlockSpec((d, tn), lambda j: (0, j)),
                  pl.BlockSpec((1, tn), lambda j: (0, j))],
        out_specs=pl.BlockSpec((rows, tn), lambda j: (0, j)),
        out_shape=jax.ShapeDtypeStruct((rows, n), F32),
        compiler_params=_cparams("arbitrary"),
        name="ada_mod",
    )(c, w_ada, b_ada.reshape(1, n))


def _inproj_kernel(x_ref, sc_ref, sh_ref, g_ref, cos_ref, sin_ref, w_ref, wba_ref, o_ref, oba_ref, h_scr,
                   *, n_rope_tiles, n_q_tiles, heads_per_tile, q_scale, norm_rows_chunk):
    j = pl.program_id(1)

    @pl.when(j == 0)
    def _():
        wba = wba_ref[...].astype(BF16)
        per_row_mod = sc_ref.shape[0] != 1

        def norm_rows(ci, carry):
            rs = pl.ds(pl.multiple_of(ci * norm_rows_chunk, norm_rows_chunk), norm_rows_chunk)
            x = x_ref[rs, :]
            ms = jnp.mean(x * x, axis=-1, keepdims=True)
            y = x * lax.rsqrt(ms + NORM_EPS) * g_ref[...]
            sc = sc_ref[rs, :] if per_row_mod else sc_ref[...]
            sh = sh_ref[rs, :] if per_row_mod else sh_ref[...]
            hb = (y * (1.0 + sc) + sh).astype(BF16)
            h_scr[rs, :] = hb
            oba_ref[rs, :] = _dot(hb, wba)
            return carry
        lax.fori_loop(0, x_ref.shape[0] // norm_rows_chunk, norm_rows, 0)

    acc = _dot(h_scr[...], w_ref[...].astype(BF16))

    @pl.when(j < n_rope_tiles)
    def _():
        scale = jnp.where(j < n_q_tiles, q_scale, 1.0).astype(F32)
        cos = cos_ref[...] * scale
        sin = sin_ref[...] * scale
        for h in range(heads_per_tile):
            xh = acc[:, h * HEAD_DIM:(h + 1) * HEAD_DIM]
            o_ref[:, h * HEAD_DIM:(h + 1) * HEAD_DIM] = xh * cos + pltpu.roll(xh, HEAD_DIM // 2, 1) * sin

    @pl.when(j >= n_rope_tiles)
    def _():
        o_ref[...] = acc


def in_proj(x, scale, shift, g, cos, sin, w_in, w_ba, *, tm, d_attn):
    rows, d = x.shape
    tm = min(tm, rows)
    tn = 512
    n_main = (w_in.shape[1] // tn) * tn
    mod_rows = scale.shape[0]
    mod_block = (tm, d) if mod_rows == rows else (1, d)
    mod_map = (lambda i, j: (i, 0)) if mod_rows == rows else (lambda i, j: (0, 0))
    kern = functools.partial(_inproj_kernel, n_rope_tiles=2 * d_attn // tn, n_q_tiles=d_attn // tn,
                             heads_per_tile=tn // HEAD_DIM, q_scale=HEAD_DIM ** -0.5, norm_rows_chunk=min(256, tm))
    return pl.pallas_call(
        kern,
        grid=(rows // tm, n_main // tn),
        in_specs=[pl.BlockSpec((tm, d), lambda i, j: (i, 0)),
                  pl.BlockSpec(mod_block, mod_map),
                  pl.BlockSpec(mod_block, mod_map),
                  pl.BlockSpec((1, d), lambda i, j: (0, 0)),
                  pl.BlockSpec((tm, HEAD_DIM), lambda i, j: (i, 0)),
                  pl.BlockSpec((tm, HEAD_DIM), lambda i, j: (i, 0)),
                  pl.BlockSpec((d, tn), lambda i, j: (0, j)),
                  pl.BlockSpec((d, LANES), lambda i, j: (0, 0))],
        out_specs=[pl.BlockSpec((tm, tn), lambda i, j: (i, j)),
                   pl.BlockSpec((tm, LANES), lambda i, j: (i, 0))],
        out_shape=[jax.ShapeDtypeStruct((rows, n_main), F32),
                   jax.ShapeDtypeStruct((rows, LANES), F32)],
        scratch_shapes=[pltpu.VMEM((tm, d), BF16)],
        compiler_params=_cparams("arbitrary", "arbitrary"),
        name="in_proj",
    )(x, scale, shift, g.reshape(1, d), cos, sin, w_in, w_ba)


ATTN_SPAN = ATTN_BLOCK * max(d for _, d in DILATED_PATTERNS)
ATTN_UNROLL = 4


def _strided_rows(start, size, stride):
    return pl.ds(start, size) if stride == 1 else pl.ds(start, size, stride=stride)


def _attn_prompt_kernel(q_ref, kp_ref, kc_ref, vp_ref, vc_ref, o_ref, kbuf, vbuf, *stat_bufs):
    n = pl.program_id(0)
    span = ATTN_SPAN
    blk = ATTN_BLOCK
    kbuf[0:span, :] = kp_ref[...]
    kbuf[span:2 * span, :] = kc_ref[...]
    vbuf[0:span, :] = vp_ref[...]
    vbuf[span:2 * span, :] = vc_ref[...]
    qi = lax.broadcasted_iota(jnp.int32, (blk, 2 * blk), 0)
    kj = lax.broadcasted_iota(jnp.int32, (blk, 2 * blk), 1)
    dist = qi + blk - kj
    bias = jnp.where(dist < 0, NEG_BIG, jnp.where(dist > blk, NEG_BIG, 0.0)).astype(F32)
    bias_first = jnp.where(kj < blk, NEG_BIG, bias)

    for pi, (_, dil) in enumerate(DILATED_PATTERNS):
        o_buf, l_buf = stat_bufs[2 * pi], stat_bufs[2 * pi + 1]
        unit_span = blk * dil

        def unit(u, carry, dil=dil, unit_span=unit_span, o_buf=o_buf, l_buf=l_buf):
            sub = u // dil
            start = sub * unit_span + (u - sub * dil)
            q = q_ref[_strided_rows(start, blk, dil), :].astype(BF16)
            kcat = kbuf[_strided_rows(span + start - unit_span, 2 * blk, dil), :].astype(BF16)
            vcat = vbuf[_strided_rows(span + start - unit_span, 2 * blk, dil), :].astype(BF16)
            first = jnp.logical_and(n == 0, sub == 0)
            s = _dot_nt(q, kcat) + jnp.where(first, bias_first, bias)
            m = jnp.max(s, axis=-1, keepdims=True)
            p = jnp.exp(s - m)
            l = jnp.sum(p, axis=-1, keepdims=True)
            o_buf[_strided_rows(start, blk, dil), :] = _dot(p.astype(BF16), vcat) / l
            l_buf[_strided_rows(start, blk, dil), :] = jnp.broadcast_to(m + jnp.log(l), (blk, HEAD_DIM))
            return carry
        lax.fori_loop(0, span // blk, unit, 0, unroll=ATTN_UNROLL)

    def merge(bi, carry):
        rs = pl.ds(pl.multiple_of(bi * blk, blk), blk)
        lses = [stat_bufs[2 * pi + 1][rs, :] for pi in range(len(DILATED_PATTERNS))]
        mx = lses[0]
        for x in lses[1:]:
            mx = jnp.maximum(mx, x)
        num = jnp.zeros((blk, HEAD_DIM), F32)
        den = jnp.zeros((blk, HEAD_DIM), F32)
        for pi, x in enumerate(lses):
            wgt = jnp.exp(x - mx)
            num = num + wgt * stat_bufs[2 * pi][rs, :]
            den = den + wgt
        o_ref[rs, :] = num / den
        return carry
    lax.fori_loop(0, span // blk, merge, 0)


def attn_prompt(p, *, n_heads):
    s = p.shape[0]
    span = ATTN_SPAN
    assert s % span == 0
    for window, dil in DILATED_PATTERNS:
        assert window // dil == ATTN_BLOCK and dil & (dil - 1) == 0
    blk = (span, HEAD_DIM)
    n_stats = 2 * len(DILATED_PATTERNS)
    return pl.pallas_call(
        _attn_prompt_kernel,
        grid=(s // span, n_heads),
        in_specs=[pl.BlockSpec(blk, lambda n, h: (n, h)),
                  pl.BlockSpec(blk, lambda n, h: (jnp.maximum(n - 1, 0), n_heads + h)),
                  pl.BlockSpec(blk, lambda n, h: (n, n_heads + h)),
                  pl.BlockSpec(blk, lambda n, h: (jnp.maximum(n - 1, 0), 2 * n_heads + h)),
                  pl.BlockSpec(blk, lambda n, h: (n, 2 * n_heads + h))],
        out_specs=pl.BlockSpec(blk, lambda n, h: (n, h)),
        out_shape=jax.ShapeDtypeStruct((s, n_heads * HEAD_DIM), F32),
        scratch_shapes=[pltpu.VMEM((2 * span, HEAD_DIM), F32), pltpu.VMEM((2 * span, HEAD_DIM), F32)]
                       + [pltpu.VMEM((span, HEAD_DIM), F32)] * n_stats,
        compiler_params=_cparams("arbitrary", "arbitrary"),
        name="attn_prompt",
    )(p, p, p, p, p)


def _attn_sample_kernel(q_ref, kn_ref, vn_ref, kf_ref, vf_ref, kc_ref, vc_ref, cntf_ref, cntc_ref, o_ref,
                        *, n_heads, t_new, half):
    mf = kf_ref.shape[1]
    near = kc_ref.shape[1] // n_heads
    cnt_f = cntf_ref[...]
    cnt_c = cntc_ref[...]
    bias_f = jnp.where(cnt_f > 0.0, 0.0, NEG_BIG)
    bias_c = jnp.where(cnt_c > 0.0, 0.0, NEG_BIG)
    ti = lax.broadcasted_iota(jnp.int32, (t_new, 1), 0)
    for h in range(n_heads):
        sl = slice(h * HEAD_DIM, (h + 1) * HEAD_DIM)
        q = q_ref[:, sl]
        kn = kn_ref[:, sl]
        vn = vn_ref[:, sl]
        q16 = q.astype(BF16)
        far_rows = pl.ds(h, half, stride=n_heads)
        k_far = kf_ref[0, :, 0, far_rows, :].reshape(mf * half, HEAD_DIM).astype(BF16)
        v_far = vf_ref[0, :, 0, far_rows, :].reshape(mf * half, HEAD_DIM).astype(BF16)
        near_rows = pl.ds(h, near, stride=n_heads)
        k_near = kc_ref[0, near_rows, :].astype(BF16)
        v_near = vc_ref[0, near_rows, :].astype(BF16)
        s_f = _dot_nt(q16, k_far) + bias_f
        s_c = _dot_nt(q16, k_near) + bias_c
        m = jnp.maximum(jnp.max(s_f, axis=-1, keepdims=True), jnp.max(s_c, axis=-1, keepdims=True))
        s_n, c_n = [], []
        for t2 in range(t_new):
            diff = ti - t2
            c = jnp.zeros((t_new, 1), F32)
            for window, dil in DILATED_PATTERNS:
                ok = (diff >= 0) & ((diff & (dil - 1)) == 0) & (diff <= window)
                c = c + jnp.where(ok, 1.0, 0.0)
            sn = jnp.where(c > 0.0, jnp.sum(q * kn[t2:t2 + 1, :], axis=-1, keepdims=True), NEG_BIG)
            m = jnp.maximum(m, sn)
            s_n.append(sn)
            c_n.append(c)
        e_f = cnt_f * jnp.exp(s_f - m)
        e_c = cnt_c * jnp.exp(s_c - m)
        l = jnp.sum(e_f, axis=-1, keepdims=True) + jnp.sum(e_c, axis=-1, keepdims=True)
        acc = _dot(e_f.astype(BF16), v_far) + _dot(e_c.astype(BF16), v_near)
        for t2 in range(t_new):
            e_n = c_n[t2] * jnp.exp(s_n[t2] - m)
            l = l + e_n
            acc = acc + e_n * vn[t2:t2 + 1, :]
        o_ref[:, sl] = acc / l


def _sample_key_counts(t_new, cache_len):
    t = np.arange(t_new)[:, None]
    p = np.arange(cache_len)[None, :]
    diff = cache_len + t - p
    cnt = np.zeros((t_new, cache_len), np.float32)
    for window, dil in DILATED_PATTERNS:
        cnt += ((diff % dil == 0) & (diff <= window)).astype(np.float32)
    return cnt


def attn_sample(p, cache_k, cache_v, *, t_new):
    bt = p.shape[0]
    b = bt // t_new
    _, cache_len, n_heads, hd = cache_k.shape
    d_attn = n_heads * hd
    d_max = max(d for _, d in DILATED_PATTERNS)
    half = d_max // 2
    near = max(w for w, d in DILATED_PATTERNS if d != d_max)
    assert hd == HEAD_DIM and cache_len >= max(w for w, _ in DILATED_PATTERNS)
    assert t_new <= half and cache_len % d_max == 0 and near % d_max == 0 and cache_len % near == 0
    for _, dil in DILATED_PATTERNS:
        assert dil & (dil - 1) == 0
    mf = (cache_len - near) // d_max
    cnt = _sample_key_counts(t_new, cache_len)
    far_pos = (np.arange(mf)[:, None] * d_max + np.arange(half)[None, :]).reshape(-1)
    covered = np.zeros(cache_len, bool)
    covered[far_pos] = True
    covered[cache_len - near:] = True
    assert not cnt[:, ~covered].any()
    cnt_far = jnp.asarray(cnt[:, far_pos])
    cnt_near = jnp.asarray(cnt[:, cache_len - near:])
    far_shape = (b, cache_len // d_max, 2, half * n_heads, hd)
    near_shape = (b, cache_len * n_heads, hd)
    kern = functools.partial(_attn_sample_kernel, n_heads=n_heads, t_new=t_new, half=half)
    blk = (t_new, d_attn)
    far_blk = pl.BlockSpec((1, mf, 1, half * n_heads, hd), lambda i: (i, 0, 0, 0, 0))
    near_blk = pl.BlockSpec((1, near * n_heads, hd), lambda i: (i, cache_len // near - 1, 0))
    return pl.pallas_call(
        kern,
        grid=(b,),
        in_specs=[pl.BlockSpec(blk, lambda i: (i, 0)),
                  pl.BlockSpec(blk, lambda i: (i, 1)),
                  pl.BlockSpec(blk, lambda i: (i, 2)),
                  far_blk, far_blk, near_blk, near_blk,
                  pl.BlockSpec(cnt_far.shape, lambda i: (0, 0)),
                  pl.BlockSpec(cnt_near.shape, lambda i: (0, 0))],
        out_specs=pl.BlockSpec(blk, lambda i: (i, 0)),
        out_shape=jax.ShapeDtypeStruct((bt, d_attn), F32),
        compiler_params=_cparams("arbitrary"),
        name="attn_sample",
    )(p, p, p, cache_k.reshape(far_shape), cache_v.reshape(far_shape),
      cache_k.reshape(near_shape), cache_v.reshape(near_shape), cnt_far, cnt_near)


def _causal_conv_silu(x, prev8, cw):
    t = x.shape[0]
    row8 = lax.broadcasted_iota(jnp.int32, (SUBLANES, 1), 0)
    conv = x * cw[CONV_W - 1:CONV_W, :]
    for i in range(1, CONV_W):
        xr = pltpu.roll(x, i, 0)
        pr = pltpu.roll(prev8, i, 0)
        head = jnp.where(row8 < i, pr, xr[:SUBLANES, :])
        xs = head if t == SUBLANES else jnp.concatenate([head, xr[SUBLANES:, :]], axis=0)
        conv = conv + xs * cw[CONV_W - 1 - i:CONV_W - i, :]
    return conv * _sigmoid(conv)


def _l2norm(x):
    return x * lax.rsqrt(jnp.sum(x * x, axis=-1, keepdims=True) + L2_EPS)


def _gated_rmsnorm(o, g_row, z):
    y = o * lax.rsqrt(jnp.mean(o * o, axis=-1, keepdims=True) + NORM_EPS) * g_row
    return y * (z * _sigmoid(z))


def _beta_and_g(ba, alog_row, dtb_row):
    beta = _sigmoid(ba)
    g = -jnp.exp(alog_row) * _softplus(ba + dtb_row)
    return beta, g


def _gdn_prompt_kernel(qkv_ref, z_ref, ba_ref, cw_ref, alog_ref, dtb_ref, gn_ref,
                       o_ref, tail_ref, ssm_ref, s_scr, prev_scr, *, n_heads):
    n = pl.program_id(0)
    c = GDN_CHUNK
    hd = HEAD_DIM

    @pl.when(n == 0)
    def _():
        s_scr[...] = jnp.zeros_like(s_scr)
        prev_scr[...] = jnp.zeros_like(prev_scr)

    x = qkv_ref[...]
    conv = _causal_conv_silu(x, prev_scr[...], cw_ref[...])
    prev_scr[...] = x[c - SUBLANES:, :]
    tail_ref[...] = x[c - SUBLANES:, :]

    beta_all, g_all = _beta_and_g(ba_ref[...], alog_ref[...], dtb_ref[...])
    ri = lax.broadcasted_iota(jnp.int32, (c, c), 0)
    ci = lax.broadcasted_iota(jnp.int32, (c, c), 1)
    tri_incl = ri >= ci
    tri_strict = ri > ci
    eye = jnp.where(ri == ci, 1.0, 0.0).astype(F32)
    blk_xor = ri ^ ci
    tri_b = jnp.where(tri_incl, 1.0, 0.0).astype(BF16)
    g_hi, g_mid, g_lo = _split3(g_all)
    gcum = _dot(tri_b, g_hi) + _dot(tri_b, g_mid) + _dot(tri_b, g_lo)
    gcum_t = gcum.T
    gn = gn_ref[...]

    hs = range(n_heads)
    q = [_l2norm(conv[:, h * hd:(h + 1) * hd]) * (hd ** -0.5) for h in hs]
    k = [_l2norm(conv[:, (n_heads + h) * hd:(n_heads + h + 1) * hd]) for h in hs]
    v = [conv[:, (2 * n_heads + h) * hd:(2 * n_heads + h + 1) * hd] for h in hs]
    beta = [beta_all[:, h:h + 1] for h in hs]
    gc = [gcum[:, n_heads + h:n_heads + h + 1] for h in hs]
    gr = [gcum_t[n_heads + h:n_heads + h + 1, :] for h in hs]
    g_last = [gcum[c - 1:c, n_heads + h:n_heads + h + 1] for h in hs]
    decay = [jnp.where(tri_incl, jnp.exp(gc[h] - gr[h]), 0.0) for h in hs]
    kb = [k[h] * beta[h] for h in hs]
    k16 = [k[h].astype(BF16) for h in hs]
    a = [jnp.where(tri_strict, _dot_nt(kb[h].astype(BF16), k16[h]) * decay[h], 0.0) for h in hs]
    qk = [_dot_nt(q[h].astype(BF16), k16[h]) * decay[h] for h in hs]
    t_inv = [eye - jnp.where((blk_xor >> 1) == 0, a[h], 0.0) for h in hs]
    for lvl in range(2, int(math.log2(c)) + 1):
        join = (blk_xor >> (lvl - 1)) == 1
        t16 = [t_inv[h].astype(BF16) for h in hs]
        lt = [_dot(jnp.where(join, a[h], 0.0).astype(BF16), t16[h]).astype(BF16) for h in hs]
        t_inv = [t_inv[h] - _dot(t16[h], lt[h]) for h in hs]
    rhs = [jnp.concatenate([v[h] * beta[h], kb[h] * jnp.exp(gc[h])], axis=-1).astype(BF16) for h in hs]
    sol = [_dot(t_inv[h].astype(BF16), rhs[h]) for h in hs]
    s_old = [s_scr[h] for h in hs]
    s16 = [s_old[h].astype(BF16) for h in hs]
    v_new = [sol[h][:, :hd] - _dot(sol[h][:, hd:].astype(BF16), s16[h]) for h in hs]
    v16 = [v_new[h].astype(BF16) for h in hs]
    o = [_dot((q[h] * jnp.exp(gc[h])).astype(BF16), s16[h]) + _dot(qk[h].astype(BF16), v16[h]) for h in hs]
    k_dec = [(k[h] * jnp.exp(g_last[h] - gc[h])).T.astype(BF16) for h in hs]
    for h in hs:
        s_scr[h] = s_old[h] * jnp.exp(g_last[h]) + _dot(k_dec[h], v16[h])
        o_ref[:, h * hd:(h + 1) * hd] = _gated_rmsnorm(o[h], gn, z_ref[:, h * hd:(h + 1) * hd])

    @pl.when(n == pl.num_programs(0) - 1)
    def _():
        ssm_ref[...] = s_scr[...]


def gdn_prompt(p, pba, conv_w, alog_row, dtb_row, g_norm, *, n_heads, qkv_col_block, z_col_block):
    s = p.shape[0]
    c = GDN_CHUNK
    hd = HEAD_DIM
    conv_dim = 3 * n_heads * hd
    kern = functools.partial(_gdn_prompt_kernel, n_heads=n_heads)
    return pl.pallas_call(
        kern,
        grid=(s // c,),
        in_specs=[pl.BlockSpec((c, conv_dim), lambda n: (n, qkv_col_block)),
                  pl.BlockSpec((c, n_heads * hd), lambda n: (n, z_col_block)),
                  pl.BlockSpec((c, LANES), lambda n: (n, 0)),
                  pl.BlockSpec((CONV_W, conv_dim), lambda n: (0, 0)),
                  pl.BlockSpec((1, LANES), lambda n: (0, 0)),
                  pl.BlockSpec((1, LANES), lambda n: (0, 0)),
                  pl.BlockSpec((1, hd), lambda n: (0, 0))],
        out_specs=[pl.BlockSpec((c, n_heads * hd), lambda n: (n, 0)),
                   pl.BlockSpec((SUBLANES, conv_dim), lambda n: (0, 0)),
                   pl.BlockSpec((n_heads, hd, hd), lambda n: (0, 0, 0))],
        out_shape=[jax.ShapeDtypeStruct((s, n_heads * hd), F32),
                   jax.ShapeDtypeStruct((SUBLANES, conv_dim), F32),
                   jax.ShapeDtypeStruct((n_heads, hd, hd), F32)],
        scratch_shapes=[pltpu.VMEM((n_heads, hd, hd), F32), pltpu.VMEM((SUBLANES, conv_dim), F32)],
        compiler_params=_cparams("arbitrary"),
        name="gdn_prompt",
    )(p, p, pba, conv_w, alog_row, dtb_row, g_norm.reshape(1, hd))


def _gdn_sample_kernel(qkv_ref, z_ref, ba_ref, prev_ref, s0_ref, cw_ref, alog_ref, dtb_ref, gn_ref,
                       o_ref, ssm_ref, *, n_heads, t_new):
    hd = HEAD_DIM
    x = qkv_ref[...]
    conv = _causal_conv_silu(x, prev_ref[0], cw_ref[...])
    beta_all, g_all = _beta_and_g(ba_ref[...], alog_ref[...], dtb_ref[...])
    eg_all = jnp.exp(g_all)
    gn = gn_ref[...]
    pad = jnp.zeros((hd - 2 * t_new, hd), F32)
    hs = range(n_heads)
    q = [_l2norm(conv[:, h * hd:(h + 1) * hd]) * (hd ** -0.5) for h in hs]
    k = [_l2norm(conv[:, (n_heads + h) * hd:(n_heads + h + 1) * hd]) for h in hs]
    v = [conv[:, (2 * n_heads + h) * hd:(2 * n_heads + h + 1) * hd] for h in hs]
    kq_t = [jnp.concatenate([k[h], q[h], pad], axis=0).T for h in hs]
    s = [s0_ref[0, h] for h in hs]
    rows = [[] for _ in hs]
    for t in range(t_new):
        for h in hs:
            kb = jnp.broadcast_to(kq_t[h][:, t:t + 1], (hd, hd))
            qb = jnp.broadcast_to(kq_t[h][:, t_new + t:t_new + t + 1], (hd, hd))
            eg = eg_all[t:t + 1, n_heads + h:n_heads + h + 1]
            bt = beta_all[t:t + 1, h:h + 1]
            sd = s[h] * eg
            ks = jnp.sum(sd * kb, axis=0, keepdims=True)
            v_new = bt * (v[h][t:t + 1, :] - ks)
            s[h] = sd + kb * v_new
            rows[h].append(jnp.sum(s[h] * qb, axis=0, keepdims=True))
    for h in hs:
        ssm_ref[0, h] = s[h]
        o = jnp.concatenate(rows[h], axis=0)
        o_ref[:, h * hd:(h + 1) * hd] = _gated_rmsnorm(o, gn, z_ref[:, h * hd:(h + 1) * hd])


def gdn_sample(p, pba, prev8, s0, conv_w, alog_row, dtb_row, g_norm, *, n_heads, t_new, qkv_col_block, z_col_block):
    bt = p.shape[0]
    b = bt // t_new
    hd = HEAD_DIM
    conv_dim = 3 * n_heads * hd
    kern = functools.partial(_gdn_sample_kernel, n_heads=n_heads, t_new=t_new)
    return pl.pallas_call(
        kern,
        grid=(b,),
        in_specs=[pl.BlockSpec((t_new, conv_dim), lambda i: (i, qkv_col_block)),
                  pl.BlockSpec((t_new, n_heads * hd), lambda i: (i, z_col_block)),
                  pl.BlockSpec((t_new, LANES), lambda i: (i, 0)),
                  pl.BlockSpec((1, SUBLANES, conv_dim), lambda i: (i, 0, 0)),
                  pl.BlockSpec((1, n_heads, hd, hd), lambda i: (i, 0, 0, 0)),
                  pl.BlockSpec((CONV_W, conv_dim), lambda i: (0, 0)),
                  pl.BlockSpec((1, LANES), lambda i: (0, 0)),
                  pl.BlockSpec((1, LANES), lambda i: (0, 0)),
                  pl.BlockSpec((1, hd), lambda i: (0, 0))],
        out_specs=[pl.BlockSpec((t_new, n_heads * hd), lambda i: (i, 0)),
                   pl.BlockSpec((1, n_heads, hd, hd), lambda i: (i, 0, 0, 0))],
        out_shape=[jax.ShapeDtypeStruct((bt, n_heads * hd), F32),
                   jax.ShapeDtypeStruct((b, n_heads, hd, hd), F32)],
        compiler_params=_cparams("arbitrary"),
        name="gdn_sample",
    )(p, p, pba, prev8, s0, conv_w, alog_row, dtb_row, g_norm.reshape(1, hd))


def _outproj_kernel(oa_ref, ob_ref, x_ref, gate_ref, sc_ref, sh_ref, gpost_ref, gpre_ref, wo_ref, wr_hi_ref, wr_lo_ref,
                    br_ref, x1_ref, h2_ref, tw_ref, ti_ref):
    d_attn = oa_ref.shape[1]
    mix = (_dot(oa_ref[...].astype(BF16), wo_ref[:d_attn, :]) + _dot(ob_ref[...].astype(BF16), wo_ref[d_attn:, :]))
    y = mix * lax.rsqrt(jnp.mean(mix * mix, axis=-1, keepdims=True) + NORM_EPS) * gpost_ref[...]
    x1 = x_ref[...] + gate_ref[...] * y
    x1_ref[...] = x1
    hn = x1 * lax.rsqrt(jnp.mean(x1 * x1, axis=-1, keepdims=True) + NORM_EPS) * gpre_ref[...]
    h2 = hn * (1.0 + sc_ref[...]) + sh_ref[...]
    h2_ref[...] = _pack_bf16_pairs(h2)

    h_hi = h2.astype(BF16)
    h_lo = (h2 - h_hi.astype(F32)).astype(BF16)
    logits = (_dot(h_hi, wr_hi_ref[...]) + _dot(h_hi, wr_lo_ref[...]) + _dot(h_lo, wr_hi_ref[...])) + br_ref[...]
    lane = lax.broadcasted_iota(jnp.int32, logits.shape, 1)
    tops, idxs = [], []
    for _ in range(TOP_K):
        mx = jnp.max(logits, axis=-1, keepdims=True)
        ix = jnp.min(jnp.where(logits == mx, lane, LANES), axis=-1, keepdims=True)
        tops.append(mx)
        idxs.append(ix)
        logits = jnp.where(lane == ix, 2.0 * NEG_BIG, logits)
    es = [jnp.exp(t - tops[0]) for t in tops]
    tot = es[0]
    for e in es[1:]:
        tot = tot + e
    tw = jnp.zeros(logits.shape, F32)
    ti = jnp.zeros(logits.shape, jnp.int32)
    for kk in range(TOP_K):
        tw = jnp.where(lane == kk, es[kk] / tot, tw)
        ti = jnp.where(lane == kk, idxs[kk], ti)
    tw_ref[...] = tw
    ti_ref[...] = ti


def out_proj(o_a, o_b, x, gate, scale, shift, g_post, g_pre, wo16, wr_hi, wr_lo, br_row, *, tm):
    rows, d = x.shape
    tm = min(tm, rows)
    d_attn = o_a.shape[1]
    mod_rows = gate.shape[0]
    mod_block = (tm, d) if mod_rows == rows else (1, d)
    mod_map = (lambda i: (i, 0)) if mod_rows == rows else (lambda i: (0, 0))
    half = pl.BlockSpec((tm, d_attn), lambda i: (i, 0))
    full = pl.BlockSpec((tm, d), lambda i: (i, 0))
    row_d = pl.BlockSpec((1, d), lambda i: (0, 0))
    lane_blk = pl.BlockSpec((tm, LANES), lambda i: (i, 0))
    in_specs = [half, pl.BlockSpec((tm, d - d_attn), lambda i: (i, 0)), full,
                pl.BlockSpec(mod_block, mod_map), pl.BlockSpec(mod_block, mod_map), pl.BlockSpec(mod_block, mod_map),
                row_d, row_d,
                pl.BlockSpec((d, d), lambda i: (0, 0)),
                pl.BlockSpec((d, LANES), lambda i: (0, 0)), pl.BlockSpec((d, LANES), lambda i: (0, 0)),
                pl.BlockSpec((1, LANES), lambda i: (0, 0))]
    args = [o_a, o_b, x, gate, scale, shift, g_post.reshape(1, d), g_pre.reshape(1, d), wo16, wr_hi, wr_lo, br_row]
    return pl.pallas_call(
        _outproj_kernel,
        grid=(rows // tm,),
        in_specs=in_specs,
        out_specs=[full, pl.BlockSpec((tm, d // 2), lambda i: (i, 0)), lane_blk, lane_blk],
        out_shape=[jax.ShapeDtypeStruct((rows, d), F32), jax.ShapeDtypeStruct((rows, d // 2), jnp.uint32),
                   jax.ShapeDtypeStruct((rows, LANES), F32), jax.ShapeDtypeStruct((rows, LANES), jnp.int32)],
        compiler_params=_cparams("arbitrary"),
        name="out_proj",
    )(*args)


def _pack_bf16_pairs(x):
    half = x.shape[1] // 2
    bits = pltpu.bitcast(x.astype(BF16).astype(F32), jnp.uint32)
    return bits[:, :half] | (bits[:, half:] >> 16)


def _unpack_bf16_pairs(w):
    hi = pltpu.bitcast(w & jnp.uint32(0xFFFF0000), F32).astype(BF16)
    lo = pltpu.bitcast(w << 16, F32).astype(BF16)
    return hi, lo


DISPATCH_TOKENS = 256
ROW_DMA_UNROLL = 8


def _dispatch_kernel(pos_ref, zs_ref, h_ref, x_hbm, zbuf, sem_z, sem):
    i = pl.program_id(0)
    tt = h_ref.shape[0]
    sb = zbuf.shape[0]
    n_items = zs_ref.shape[0]
    n_sub = MOE_BLOCK_ROWS // sb

    def zero_copy(w, s):
        row0 = pl.multiple_of((w * n_sub + s) * sb, sb)
        return pltpu.make_async_copy(zbuf, x_hbm.at[pl.ds(row0, sb), :], sem_z)

    @pl.when(i == 0)
    def _():
        zbuf[...] = jnp.zeros(zbuf.shape, zbuf.dtype)

        def start(w, carry):
            lax.fori_loop(zs_ref[w], n_sub, lambda s, c: (zero_copy(w, s).start(), c)[1], 0)
            return carry
        lax.fori_loop(0, n_items, start, 0)

        def wait(w, carry):
            lax.fori_loop(zs_ref[w], n_sub, lambda s, c: (zero_copy(w, s).wait(), c)[1], 0)
            return carry
        lax.fori_loop(0, n_items, wait, 0)

    base = i * (tt * TOP_K)

    def row_copy(t, kk):
        dst = pos_ref[base + t * TOP_K + kk]
        return pltpu.make_async_copy(h_ref.at[pl.ds(t, 1), :], x_hbm.at[pl.ds(dst, 1), :], sem)

    for kk in range(TOP_K):
        def start(t, carry, kk=kk):
            row_copy(t, kk).start()
            return carry
        lax.fori_loop(0, tt, start, 0, unroll=ROW_DMA_UNROLL)
    for kk in range(TOP_K):
        def wait(t, carry, kk=kk):
            row_copy(t, kk).wait()
            return carry
        lax.fori_loop(0, tt, wait, 0, unroll=ROW_DMA_UNROLL)


def moe_dispatch(h2p, pos_flat, zero_from):
    n_tok, half = h2p.shape
    tt = math.gcd(DISPATCH_TOKENS, n_tok)
    assert tt % SUBLANES == 0
    n_items = zero_from.shape[0]
    grid_spec = pltpu.PrefetchScalarGridSpec(
        num_scalar_prefetch=2,
        grid=(n_tok // tt,),
        in_specs=[pl.BlockSpec((tt, half), lambda i, pos, zs: (i, 0))],
        out_specs=pl.BlockSpec(memory_space=pl.ANY),
        scratch_shapes=[pltpu.VMEM((MOE_SUB_ROWS, half), jnp.uint32), pltpu.SemaphoreType.DMA(()),
                        pltpu.SemaphoreType.DMA(())],
    )
    return pl.pallas_call(
        _dispatch_kernel,
        grid_spec=grid_spec,
        out_shape=jax.ShapeDtypeStruct((n_items * MOE_BLOCK_ROWS, half), jnp.uint32),
        compiler_params=_cparams("arbitrary"),
        name="moe_dispatch",
    )(pos_flat, zero_from, h2p)


def _moe_kernel(e_ref, ns_ref, x_ref, wg_ref, wu_ref, bg_ref, bu_ref, wd_ref, bd_ref, o_ref, xb_scr):
    w = pl.program_id(0)
    j = pl.program_id(1)
    ns = ns_ref[w]
    sb = MOE_SUB_ROWS
    n_sub = MOE_BLOCK_ROWS // sb
    half = x_ref.shape[1]

    @pl.when(j == 0)
    def _():
        def zero(s, carry):
            r0 = pl.multiple_of(s * sb, sb)
            o_ref[pl.ds(r0, sb), :] = jnp.zeros((sb, o_ref.shape[1]), F32)
            return carry
        lax.fori_loop(ns, n_sub, zero, 0)

        def unpack(s, carry):
            rs = pl.ds(pl.multiple_of(s * sb, sb), sb)
            hi, lo = _unpack_bf16_pairs(x_ref[rs, :])
            xb_scr[rs, :half] = hi
            xb_scr[rs, half:] = lo
            return carry
        lax.fori_loop(0, ns, unpack, 0)

    @pl.when(ns > 0)
    def _():
        wg = wg_ref[0].astype(BF16)
        wu = wu_ref[0].astype(BF16)
        wd = wd_ref[0].astype(BF16)
        bg = bg_ref[0]
        bu = bu_ref[0]
        bd = bd_ref[0]

        def compute(n_rows):
            xs = xb_scr[0:n_rows, :]
            gate = jnp.minimum(_dot(xs, wg) + bg, SWIGLU_LIMIT)
            up = jnp.clip(_dot(xs, wu) + bu, -SWIGLU_LIMIT, SWIGLU_LIMIT)
            act = (gate * _sigmoid(SWIGLU_ALPHA * gate) * (up + 1.0)).astype(BF16)
            dc = MOE_DOWN_CHUNK
            for c0 in range(0, o_ref.shape[1], dc):
                y = _dot(act, wd[:, c0:c0 + dc])

                @pl.when(j == 0)
                def _():
                    o_ref[0:n_rows, c0:c0 + dc] = y + bd[:, c0:c0 + dc]

                @pl.when(j > 0)
                def _():
                    o_ref[0:n_rows, c0:c0 + dc] += y

        for n in range(1, n_sub + 1):
            pl.when(ns == n)(functools.partial(compute, n * sb))


def moe_experts(x_sorted, item_e, item_ns, w_gate_up, b_gate_up, w_down, b_down):
    n_rows, half = x_sorted.shape
    d = 2 * half
    n_e, _, two_ff = w_gate_up.shape
    d_ff = two_ff // 2
    bm = MOE_BLOCK_ROWS
    tf = MOE_FF_TILE
    n_f = d_ff // tf
    n_items = n_rows // bm

    def jj(j, ns, w):
        return jnp.where(ns[w] > 0, j, n_f - 1)

    grid_spec = pltpu.PrefetchScalarGridSpec(
        num_scalar_prefetch=2,
        grid=(n_items, n_f),
        in_specs=[pl.BlockSpec((bm, half), lambda w, j, e, ns: (w, 0)),
                  pl.BlockSpec((1, d, tf), lambda w, j, e, ns: (e[w], 0, jj(j, ns, w))),
                  pl.BlockSpec((1, d, tf), lambda w, j, e, ns: (e[w], 0, n_f + jj(j, ns, w))),
                  pl.BlockSpec((1, 1, tf), lambda w, j, e, ns: (e[w], 0, jj(j, ns, w))),
                  pl.BlockSpec((1, 1, tf), lambda w, j, e, ns: (e[w], 0, n_f + jj(j, ns, w))),
                  pl.BlockSpec((1, tf, d), lambda w, j, e, ns: (e[w], jj(j, ns, w), 0)),
                  pl.BlockSpec((1, 1, d), lambda w, j, e, ns: (e[w], 0, 0))],
        out_specs=pl.BlockSpec((bm, d), lambda w, j, e, ns: (w, 0)),
        scratch_shapes=[pltpu.VMEM((bm, d), BF16)],
    )
    return pl.pallas_call(
        _moe_kernel,
        grid_spec=grid_spec,
        out_shape=jax.ShapeDtypeStruct((n_rows, d), F32),
        compiler_params=_cparams("arbitrary", "arbitrary"),
        name="moe_experts",
    )(item_e, item_ns, x_sorted, w_gate_up, w_gate_up, b_gate_up.reshape(n_e, 1, two_ff),
      b_gate_up.reshape(n_e, 1, two_ff), w_down, b_down.reshape(n_e, 1, d))


def moe_routing(top_e):
    n_tok = top_e.shape[0]
    n_asg = n_tok * TOP_K
    bm = MOE_BLOCK_ROWS
    sb = MOE_SUB_ROWS
    n_items = n_asg // bm + N_EXPERTS
    i32 = jnp.int32
    onehot = jnp.sum((top_e[:, :, None] == jnp.arange(N_EXPERTS, dtype=i32)[None, None, :]).astype(i32), axis=1)
    counts = jnp.sum(onehot, axis=0)
    earlier = jnp.cumsum(onehot, axis=0) - onehot
    rank = jnp.take_along_axis(earlier, top_e, axis=1)
    n_it = (counts + bm - 1) // bm
    per = ((counts + jnp.maximum(n_it, 1) - 1) // jnp.maximum(n_it, 1) + sb - 1) // sb * sb
    per = jnp.maximum(per, sb)
    it_end = jnp.cumsum(n_it)
    it_base = it_end - n_it
    per_t = per[top_e]
    k_idx = rank // per_t
    pos = ((it_base[top_e] + k_idx) * bm + rank - k_idx * per_t).astype(i32)
    w = jnp.arange(n_items, dtype=i32)
    total = it_end[-1]
    e_raw = jnp.minimum(jnp.sum((it_end[None, :] <= w[:, None]).astype(i32), axis=1), N_EXPERTS - 1)
    rows = jnp.clip(counts[e_raw] - (w - it_base[e_raw]) * per[e_raw], 0, per[e_raw])
    rows = jnp.where(w < total, rows, 0)
    item_ns = (rows + sb - 1) // sb
    item_e = jnp.where(w < total, e_raw, e_raw[jnp.maximum(total - 1, 0)])
    zero_from = jnp.maximum(item_ns - 1, 0)
    return pos, item_e.astype(i32), item_ns.astype(i32), zero_from.astype(i32)


def _finish_kernel(pos_ref, x1_ref, tw_ref, gate_ref, g_ref, y_hbm, o_ref, ybuf, sem):
    i = pl.program_id(0)
    n_tiles = pl.num_programs(0)
    tm, d = x1_ref.shape

    def row_copy(base, slot, kk, t):
        src = pos_ref[base + t * TOP_K + kk]
        return pltpu.make_async_copy(y_hbm.at[pl.ds(src, 1), :], ybuf.at[slot, kk, pl.ds(t, 1), :], sem.at[slot])

    def issue(tile, slot):
        base = tile * (tm * TOP_K)
        for kk in range(TOP_K):
            def body(t, carry, kk=kk):
                row_copy(base, slot, kk, t).start()
                return carry
            lax.fori_loop(0, tm, body, 0, unroll=ROW_DMA_UNROLL)

    def wait(tile, slot):
        base = tile * (tm * TOP_K)
        for kk in range(TOP_K):
            def body(t, carry, kk=kk):
                row_copy(base, slot, kk, t).wait()
                return carry
            lax.fori_loop(0, tm, body, 0, unroll=ROW_DMA_UNROLL)

    @pl.when(i == 0)
    def _():
        issue(0, 0)

    @pl.when(i + 1 < n_tiles)
    def _():
        issue(i + 1, (i + 1) % 2)

    slot = i % 2
    wait(i, slot)
    tw = tw_ref[...]
    ff = tw[:, 0:1] * ybuf[slot, 0]
    for kk in range(1, TOP_K):
        ff = ff + tw[:, kk:kk + 1] * ybuf[slot, kk]
    y = ff * lax.rsqrt(jnp.mean(ff * ff, axis=-1, keepdims=True) + NORM_EPS) * g_ref[...]
    o_ref[...] = x1_ref[...] + gate_ref[...] * y


def finish(x1, y_sorted, pos, tw, gate, g_post, *, tm):
    rows, d = x1.shape
    tm = min(tm, rows)
    mod_rows = gate.shape[0]
    mod_block = (tm, d) if mod_rows == rows else (1, d)
    mod_map = (lambda i, pos: (i, 0)) if mod_rows == rows else (lambda i, pos: (0, 0))
    grid_spec = pltpu.PrefetchScalarGridSpec(
        num_scalar_prefetch=1,
        grid=(rows // tm,),
        in_specs=[pl.BlockSpec((tm, d), lambda i, pos: (i, 0)),
                  pl.BlockSpec((tm, LANES), lambda i, pos: (i, 0)),
                  pl.BlockSpec(mod_block, mod_map),
                  pl.BlockSpec((1, d), lambda i, pos: (0, 0)),
                  pl.BlockSpec(memory_space=pl.ANY)],
        out_specs=pl.BlockSpec((tm, d), lambda i, pos: (i, 0)),
        scratch_shapes=[pltpu.VMEM((2, TOP_K, tm, d), F32), pltpu.SemaphoreType.DMA((2,))],
    )
    return pl.pallas_call(
        _finish_kernel,
        grid_spec=grid_spec,
        out_shape=jax.ShapeDtypeStruct((rows, d), F32),
        compiler_params=_cparams("arbitrary"),
        name="finish",
    )(pos, x1, tw, gate, g_post.reshape(1, d), y_sorted)


def _rope_tables(pos):
    half = HEAD_DIM // 2
    inv_freq = ROPE_THETA ** (-jnp.arange(half, dtype=F32) / half)
    ang = pos.astype(F32)[:, None] * inv_freq[None, :]
    cos = jnp.cos(ang)
    sin = jnp.sin(ang)
    return jnp.concatenate([cos, cos], axis=-1), jnp.concatenate([-sin, sin], axis=-1)


def _lane_row(vals, offset):
    return jnp.zeros((1, LANES), F32).at[0, offset:offset + vals.shape[0]].set(vals.astype(F32))


def kernel(x_prompt, x_sample, c_prompt, c_sample, cache_win_k, cache_win_v, state_conv, state_ssm, w_ada, b_ada, g_pre_mix, g_post_mix, g_pre_ffn, g_post_ffn, w_in, conv_w, a_log, dt_bias, g_gdn_norm, w_out, w_router, b_router, w_gate_up, b_gate_up, w_down, b_down):
    depth = w_ada.shape[0]
    assert depth == 1 and x_prompt.shape[0] == 1
    _, s_len, d = x_prompt.shape
    b_s, t_s, _ = x_sample.shape
    n_heads_a = cache_win_k.shape[3]
    d_attn = n_heads_a * HEAD_DIM
    n_heads_b = state_ssm.shape[2]
    conv_dim = 3 * n_heads_b * HEAD_DIM
    n_samp = b_s * t_s
    win_p = min(max(w for w, _ in DILATED_PATTERNS), s_len)

    c_all = jnp.concatenate([c_prompt, c_sample], axis=0)
    c_rows = -(-c_all.shape[0] // SUBLANES) * SUBLANES
    c_all = jnp.pad(c_all, ((0, c_rows - c_all.shape[0]), (0, 0)))
    mod = ada_mod(c_all, w_ada[0], b_ada[0])
    mod_p = [mod[0:1, i * d:(i + 1) * d] for i in range(6)]
    mod_s = [jnp.repeat(mod[1:1 + b_s, i * d:(i + 1) * d], t_s, axis=0) for i in range(6)]

    n_main = d_attn * 3 + conv_dim + n_heads_b * HEAD_DIM
    w_ba = jnp.pad(w_in[0][:, n_main:], ((0, 0), (0, LANES - (w_in.shape[2] - n_main))))
    cos_p, sin_p = _rope_tables(jnp.arange(s_len))
    cos_s, sin_s = _rope_tables(PAST_LEN + (jnp.arange(n_samp) % t_s))
    xp = x_prompt[0]
    xs = x_sample.reshape(n_samp, d)
    p_p, pba_p = in_proj(xp, mod_p[1], mod_p[0], g_pre_mix[0], cos_p, sin_p, w_in[0], w_ba, tm=1024, d_attn=d_attn)
    p_s, pba_s = in_proj(xs, mod_s[1], mod_s[0], g_pre_mix[0], cos_s, sin_s, w_in[0], w_ba, tm=512, d_attn=d_attn)

    oa_p = attn_prompt(p_p, n_heads=n_heads_a)
    oa_s = attn_sample(p_s, cache_win_k[0], cache_win_v[0], t_new=t_s)

    alog_row = _lane_row(a_log[0], n_heads_b)
    dtb_row = _lane_row(dt_bias[0], n_heads_b)
    qkv_blk = (3 * d_attn) // conv_dim
    z_blk = (3 * d_attn + conv_dim) // (n_heads_b * HEAD_DIM)
    assert qkv_blk * conv_dim == 3 * d_attn and z_blk * n_heads_b * HEAD_DIM == 3 * d_attn + conv_dim
    ob_p, tail_p, ssm_p = gdn_prompt(p_p, pba_p, conv_w[0], alog_row, dtb_row, g_gdn_norm[0],
                                     n_heads=n_heads_b, qkv_col_block=qkv_blk, z_col_block=z_blk)
    prev8 = jnp.pad(state_conv[0], ((0, 0), (SUBLANES - (CONV_W - 1), 0), (0, 0)))
    ob_s, ssm_s = gdn_sample(p_s, pba_s, prev8, state_ssm[0], conv_w[0], alog_row, dtb_row, g_gdn_norm[0],
                             n_heads=n_heads_b, t_new=t_s, qkv_col_block=qkv_blk, z_col_block=z_blk)

    wo16 = w_out[0].astype(BF16)
    wr = jnp.pad(w_router[0], ((0, 0), (0, LANES - N_EXPERTS)))
    wr_hi = wr.astype(BF16)
    wr_lo = (wr - wr_hi.astype(F32)).astype(BF16)
    br_row = jnp.full((1, LANES), NEG_BIG, F32).at[0, :N_EXPERTS].set(b_router[0])
    x1_p, h2_p, tw_p, ti_p = out_proj(oa_p, ob_p, xp, mod_p[2], mod_p[4], mod_p[3], g_post_mix[0], g_pre_ffn[0],
                                      wo16, wr_hi, wr_lo, br_row, tm=256)
    x1_s, h2_s, tw_s, ti_s = out_proj(oa_s, ob_s, xs, mod_s[2], mod_s[4], mod_s[3], g_post_mix[0], g_pre_ffn[0],
                                      wo16, wr_hi, wr_lo, br_row, tm=256)

    h2 = jnp.concatenate([h2_p, h2_s], axis=0)
    top_e = jnp.concatenate([ti_p[:, :TOP_K], ti_s[:, :TOP_K]], axis=0)
    pos, item_e, item_ns, zero_from = moe_routing(top_e)
    x_sorted = moe_dispatch(h2, pos.reshape(-1), zero_from)
    y_sorted = moe_experts(x_sorted, item_e, item_ns, w_gate_up[0], b_gate_up[0], w_down[0], b_down[0])
    y_p = finish(x1_p, y_sorted, pos[:s_len].reshape(-1), tw_p, mod_p[5], g_post_ffn[0], tm=128)
    y_s = finish(x1_s, y_sorted, pos[s_len:].reshape(-1), tw_s, mod_s[5], g_post_ffn[0], tm=128)

    hd = HEAD_DIM
    k_p = p_p[s_len - win_p:, d_attn:2 * d_attn].reshape(1, 1, win_p, n_heads_a, hd)
    v_p = p_p[s_len - win_p:, 2 * d_attn:3 * d_attn].reshape(1, 1, win_p, n_heads_a, hd)
    conv_p = tail_p[SUBLANES - (CONV_W - 1):].reshape(1, 1, CONV_W - 1, conv_dim)
    k_s = p_s[:, d_attn:2 * d_attn].reshape(1, b_s, t_s, n_heads_a, hd)
    v_s = p_s[:, 2 * d_attn:3 * d_attn].reshape(1, b_s, t_s, n_heads_a, hd)
    conv_s = p_s.reshape(b_s, t_s, -1)[:, t_s - (CONV_W - 1):, 3 * d_attn:3 * d_attn + conv_dim][None]
    return (y_p[None], y_s.reshape(b_s, t_s, d), k_p, v_p, conv_p, ssm_p[None, None],
            k_s, v_s, conv_s, ssm_s[None])
```

```python
import functools
import math

import numpy as np
import jax
import jax.numpy as jnp
from jax import lax
from jax.experimental import pallas as pl
from jax.experimental.pallas import tpu as pltpu

F32 = jnp.float32
BF16 = jnp.bfloat16

HEAD_DIM = 128
DILATED_PATTERNS = ((128, 1), (512, 4), (2048, 16))
ATTN_BLOCK = 128
PAST_LEN = 2048
ROPE_THETA = 10000.0
CONV_W = 4
N_EXPERTS = 32
TOP_K = 4
SWIGLU_LIMIT = 7.0
SWIGLU_ALPHA = 1.702
NORM_EPS = 1e-6
L2_EPS = 1e-6

LANES = 128
SUBLANES = 8
VMEM_LIMIT_BYTES = 56 * 1024 * 1024

NEG_BIG = -1e30

GDN_CHUNK = 128
GDN_SAMPLE_SEQS = 2
MOE_BLOCK_ROWS = 1024
MOE_SUB_ROWS = 128
MOE_FF_TILE = 256
MOE_DOWN_CHUNK = 512


def _cparams(*sem):
    return pltpu.CompilerParams(dimension_semantics=sem, vmem_limit_bytes=VMEM_LIMIT_BYTES)


def _sigmoid(x):
    return 1.0 / (1.0 + jnp.exp(-x))


def _softplus(x):
    return jnp.maximum(x, 0.0) + jnp.log(1.0 + jnp.exp(-jnp.abs(x)))


def _dot(a, b):
    return jnp.dot(a, b, preferred_element_type=F32)


def _dot_nt(a, b):
    return lax.dot_general(a, b, (((1,), (1,)), ((), ())), preferred_element_type=F32)


def _split3(x):
    hi = x.astype(BF16)
    r = x - hi.astype(F32)
    mid = r.astype(BF16)
    lo = (r - mid.astype(F32)).astype(BF16)
    return hi, mid, lo


def _ada_kernel(c_ref, w_ref, b_ref, o_ref):
    c = c_ref[...]
    a = (c * _sigmoid(c)).astype(BF16)
    o_ref[...] = _dot(a, w_ref[...].astype(BF16)) + b_ref[...]


def ada_mod(c, w_ada, b_ada):
    rows, d = c.shape
    n = w_ada.shape[1]
    tn = 1024
    return pl.pallas_call(
        _ada_kernel,
        grid=(n // tn,),
        in_specs=[pl.BlockSpec((rows, d), lambda j: (0, 0)),
                  pl.BlockSpec((d, tn), lambda j: (0, j)),
                  pl.BlockSpec((1, tn), lambda j: (0, j))],
        out_specs=pl.BlockSpec((rows, tn), lambda j: (0, j)),
        out_shape=jax.ShapeDtypeStruct((rows, n), F32),
        compiler_params=_cparams("arbitrary"),
        name="ada_mod",
    )(c, w_ada, b_ada.reshape(1, n))


def _inproj_kernel(x_ref, sc_ref, sh_ref, g_ref, cos_ref, sin_ref, w_ref, wba_ref, o_ref, oba_ref, h_scr,
                   *, n_rope_tiles, n_q_tiles, heads_per_tile, q_scale, norm_rows_chunk):
    j = pl.program_id(1)

    @pl.when(j == 0)
    def _():
        wba = wba_ref[...].astype(BF16)
        per_row_mod = sc_ref.shape[0] != 1

        def norm_rows(ci, carry):
            rs = pl.ds(pl.multiple_of(ci * norm_rows_chunk, norm_rows_chunk), norm_rows_chunk)
            x = x_ref[rs, :]
            ms = jnp.mean(x * x, axis=-1, keepdims=True)
            y = x * lax.rsqrt(ms + NORM_EPS) * g_ref[...]
            sc = sc_ref[rs, :] if per_row_mod else sc_ref[...]
            sh = sh_ref[rs, :] if per_row_mod else sh_ref[...]
            hb = (y * (1.0 + sc) + sh).astype(BF16)
            h_scr[rs, :] = hb
            oba_ref[rs, :] = _dot(hb, wba)
            return carry
        lax.fori_loop(0, x_ref.shape[0] // norm_rows_chunk, norm_rows, 0)

    acc = _dot(h_scr[...], w_ref[...].astype(BF16))

    @pl.when(j < n_rope_tiles)
    def _():
        scale = jnp.where(j < n_q_tiles, q_scale, 1.0).astype(F32)
        cos = cos_ref[...] * scale
        sin = sin_ref[...] * scale
        for h in range(heads_per_tile):
            xh = acc[:, h * HEAD_DIM:(h + 1) * HEAD_DIM]
            o_ref[:, h * HEAD_DIM:(h + 1) * HEAD_DIM] = xh * cos + pltpu.roll(xh, HEAD_DIM // 2, 1) * sin

    @pl.when(j >= n_rope_tiles)
    def _():
        o_ref[...] = acc


def in_proj(x, scale, shift, g, cos, sin, w_in, w_ba, *, tm, d_attn):
    rows, d = x.shape
    tm = min(tm, rows)
    tn = 512
    n_main = (w_in.shape[1] // tn) * tn
    mod_rows = scale.shape[0]
    mod_block = (tm, d) if mod_rows == rows else (1, d)
    mod_map = (lambda i, j: (i, 0)) if mod_rows == rows else (lambda i, j: (0, 0))
    kern = functools.partial(_inproj_kernel, n_rope_tiles=2 * d_attn // tn, n_q_tiles=d_attn // tn,
                             heads_per_tile=tn // HEAD_DIM, q_scale=HEAD_DIM ** -0.5, norm_rows_chunk=min(256, tm))
    return pl.pallas_call(
        kern,
        grid=(rows // tm, n_main // tn),
        in_specs=[pl.BlockSpec((tm, d), lambda i, j: (i, 0)),
                  pl.BlockSpec(mod_block, mod_map),
                  pl.BlockSpec(mod_block, mod_map),
                  pl.BlockSpec((1, d), lambda i, j: (0, 0)),
                  pl.BlockSpec((tm, HEAD_DIM), lambda i, j: (i, 0)),
                  pl.BlockSpec((tm, HEAD_DIM), lambda i, j: (i, 0)),
                  pl.BlockSpec((d, tn), lambda i, j: (0, j)),
                  pl.BlockSpec((d, LANES), lambda i, j: (0, 0))],
        out_specs=[pl.BlockSpec((tm, tn), lambda i, j: (i, j)),
                   pl.BlockSpec((tm, LANES), lambda i, j: (i, 0))],
        out_shape=[jax.ShapeDtypeStruct((rows, n_main), F32),
                   jax.ShapeDtypeStruct((rows, LANES), F32)],
        scratch_shapes=[pltpu.VMEM((tm, d), BF16)],
        compiler_params=_cparams("arbitrary", "arbitrary"),
        name="in_proj",
    )(x, scale, shift, g.reshape(1, d), cos, sin, w_in, w_ba)


ATTN_SPAN = ATTN_BLOCK * max(d for _, d in DILATED_PATTERNS)
ATTN_UNROLL = 16


def _strided_rows(start, size, stride):
    return pl.ds(start, size) if stride == 1 else pl.ds(start, size, stride=stride)


def _attn_prompt_kernel(q_ref, kp_ref, kc_ref, vp_ref, vc_ref, o_ref, kbuf, vbuf, *stat_bufs):
    n = pl.program_id(0)
    span = ATTN_SPAN
    blk = ATTN_BLOCK
    kbuf[0:span, :] = kp_ref[...]
    kbuf[span:2 * span, :] = kc_ref[...]
    vbuf[0:span, :] = vp_ref[...]
    vbuf[span:2 * span, :] = vc_ref[...]
    qi = lax.broadcasted_iota(jnp.int32, (blk, 2 * blk), 0)
    kj = lax.broadcasted_iota(jnp.int32, (blk, 2 * blk), 1)
    dist = qi + blk - kj
    bias = jnp.where(dist < 0, NEG_BIG, jnp.where(dist > blk, NEG_BIG, 0.0)).astype(F32)
    bias_first = jnp.where(kj < blk, NEG_BIG, bias)

    for pi, (_, dil) in enumerate(DILATED_PATTERNS):
        o_buf, l_buf = stat_bufs[2 * pi], stat_bufs[2 * pi + 1]
        unit_span = blk * dil

        def unit(u, carry, dil=dil, unit_span=unit_span, o_buf=o_buf, l_buf=l_buf):
            sub = u // dil
            start = sub * unit_span + (u - sub * dil)
            q = q_ref[_strided_rows(start, blk, dil), :].astype(BF16)
            kcat = kbuf[_strided_rows(span + start - unit_span, 2 * blk, dil), :].astype(BF16)
            vcat = vbuf[_strided_rows(span + start - unit_span, 2 * blk, dil), :].astype(BF16)
            first = jnp.logical_and(n == 0, sub == 0)
            s = _dot_nt(q, kcat) + jnp.where(first, bias_first, bias)
            m = jnp.max(s, axis=-1, keepdims=True)
            p = jnp.exp(s - m)
            l = jnp.sum(p, axis=-1, keepdims=True)
            o_buf[_strided_rows(start, blk, dil), :] = _dot(p.astype(BF16), vcat) / l
            l_buf[_strided_rows(start, blk, dil), :] = jnp.broadcast_to(m + jnp.log(l), (blk, HEAD_DIM))
            return carry
        lax.fori_loop(0, span // blk, unit, 0, unroll=ATTN_UNROLL)

    def merge(bi, carry):
        rs = pl.ds(pl.multiple_of(bi * blk, blk), blk)
        lses = [stat_bufs[2 * pi + 1][rs, :] for pi in range(len(DILATED_PATTERNS))]
        mx = lses[0]
        for x in lses[1:]:
            mx = jnp.maximum(mx, x)
        num = jnp.zeros((blk, HEAD_DIM), F32)
        den = jnp.zeros((blk, HEAD_DIM), F32)
        for pi, x in enumerate(lses):
            wgt = jnp.exp(x - mx)
            num = num + wgt * stat_bufs[2 * pi][rs, :]
            den = den + wgt
        o_ref[rs, :] = num / den
        return carry
    lax.fori_loop(0, span // blk, merge, 0)


def attn_prompt(p, *, n_heads):
    s = p.shape[0]
    span = ATTN_SPAN
    assert s % span == 0
    for window, dil in DILATED_PATTERNS:
        assert window // dil == ATTN_BLOCK and dil & (dil - 1) == 0
    blk = (span, HEAD_DIM)
    n_stats = 2 * len(DILATED_PATTERNS)
    return pl.pallas_call(
        _attn_prompt_kernel,
        grid=(s // span, n_heads),
        in_specs=[pl.BlockSpec(blk, lambda n, h: (n, h)),
                  pl.BlockSpec(blk, lambda n, h: (jnp.maximum(n - 1, 0), n_heads + h)),
                  pl.BlockSpec(blk, lambda n, h: (n, n_heads + h)),
                  pl.BlockSpec(blk, lambda n, h: (jnp.maximum(n - 1, 0), 2 * n_heads + h)),
                  pl.BlockSpec(blk, lambda n, h: (n, 2 * n_heads + h))],
        out_specs=pl.BlockSpec(blk, lambda n, h: (n, h)),
        out_shape=jax.ShapeDtypeStruct((s, n_heads * HEAD_DIM), F32),
        scratch_shapes=[pltpu.VMEM((2 * span, HEAD_DIM), F32), pltpu.VMEM((2 * span, HEAD_DIM), F32)]
                       + [pltpu.VMEM((span, HEAD_DIM), F32)] * n_stats,
        compiler_params=_cparams("arbitrary", "arbitrary"),
        name="attn_prompt",
    )(p, p, p, p, p)


def _attn_sample_kernel(q_ref, kn_ref, vn_ref, kf_ref, vf_ref, kc_ref, vc_ref, cntf_ref, cntc_ref, o_ref,
                        *, n_heads, t_new, half):
    mf = kf_ref.shape[1]
    near = kc_ref.shape[1] // n_heads
    cnt_f = cntf_ref[...]
    cnt_c = cntc_ref[...]
    bias_f = jnp.where(cnt_f > 0.0, 0.0, NEG_BIG)
    bias_c = jnp.where(cnt_c > 0.0, 0.0, NEG_BIG)
    ti = lax.broadcasted_iota(jnp.int32, (t_new, 1), 0)
    for h in range(n_heads):
        sl = slice(h * HEAD_DIM, (h + 1) * HEAD_DIM)
        q = q_ref[:, sl]
        kn = kn_ref[:, sl]
        vn = vn_ref[:, sl]
        q16 = q.astype(BF16)
        far_rows = pl.ds(h, half, stride=n_heads)
        k_far = kf_ref[0, :, 0, far_rows, :].reshape(mf * half, HEAD_DIM).astype(BF16)
        v_far = vf_ref[0, :, 0, far_rows, :].reshape(mf * half, HEAD_DIM).astype(BF16)
        near_rows = pl.ds(h, near, stride=n_heads)
        k_near = kc_ref[0, near_rows, :].astype(BF16)
        v_near = vc_ref[0, near_rows, :].astype(BF16)
        s_f = _dot_nt(q16, k_far) + bias_f
        s_c = _dot_nt(q16, k_near) + bias_c
        m = jnp.maximum(jnp.max(s_f, axis=-1, keepdims=True), jnp.max(s_c, axis=-1, keepdims=True))
        s_n, c_n = [], []
        for t2 in range(t_new):
            diff = ti - t2
            c = jnp.zeros((t_new, 1), F32)
            for window, dil in DILATED_PATTERNS:
                ok = (diff >= 0) & ((diff & (dil - 1)) == 0) & (diff <= window)
                c = c + jnp.where(ok, 1.0, 0.0)
            sn = jnp.where(c > 0.0, jnp.sum(q * kn[t2:t2 + 1, :], axis=-1, keepdims=True), NEG_BIG)
            m = jnp.maximum(m, sn)
            s_n.append(sn)
            c_n.append(c)
        e_f = cnt_f * jnp.exp(s_f - m)
        e_c = cnt_c * jnp.exp(s_c - m)
        l = jnp.sum(e_f, axis=-1, keepdims=True) + jnp.sum(e_c, axis=-1, keepdims=True)
        acc = _dot(e_f.astype(BF16), v_far) + _dot(e_c.astype(BF16), v_near)
        for t2 in range(t_new):
            e_n = c_n[t2] * jnp.exp(s_n[t2] - m)
            l = l + e_n
            acc = acc + e_n * vn[t2:t2 + 1, :]
        o_ref[:, sl] = acc / l


def _sample_key_counts(t_new, cache_len):
    t = np.arange(t_new)[:, None]
    p = np.arange(cache_len)[None, :]
    diff = cache_len + t - p
    cnt = np.zeros((t_new, cache_len), np.float32)
    for window, dil in DILATED_PATTERNS:
        cnt += ((diff % dil == 0) & (diff <= window)).astype(np.float32)
    return cnt


def attn_sample(p, cache_k, cache_v, *, t_new):
    bt = p.shape[0]
    b = bt // t_new
    _, cache_len, n_heads, hd = cache_k.shape
    d_attn = n_heads * hd
    d_max = max(d for _, d in DILATED_PATTERNS)
    half = d_max // 2
    near = max(w for w, d in DILATED_PATTERNS if d != d_max)
    assert hd == HEAD_DIM and cache_len >= max(w for w, _ in DILATED_PATTERNS)
    assert t_new <= half and cache_len % d_max == 0 and near % d_max == 0 and cache_len % near == 0
    for _, dil in DILATED_PATTERNS:
        assert dil & (dil - 1) == 0
    mf = (cache_len - near) // d_max
    cnt = _sample_key_counts(t_new, cache_len)
    far_pos = (np.arange(mf)[:, None] * d_max + np.arange(half)[None, :]).reshape(-1)
    covered = np.zeros(cache_len, bool)
    covered[far_pos] = True
    covered[cache_len - near:] = True
    assert not cnt[:, ~covered].any()
    cnt_far = jnp.asarray(cnt[:, far_pos])
    cnt_near = jnp.asarray(cnt[:, cache_len - near:])
    far_shape = (b, cache_len // d_max, 2, half * n_heads, hd)
    near_shape = (b, cache_len * n_heads, hd)
    kern = functools.partial(_attn_sample_kernel, n_heads=n_heads, t_new=t_new, half=half)
    blk = (t_new, d_attn)
    far_blk = pl.BlockSpec((1, mf, 1, half * n_heads, hd), lambda i: (i, 0, 0, 0, 0))
    near_blk = pl.BlockSpec((1, near * n_heads, hd), lambda i: (i, cache_len // near - 1, 0))
    return pl.pallas_call(
        kern,
        grid=(b,),
        in_specs=[pl.BlockSpec(blk, lambda i: (i, 0)),
                  pl.BlockSpec(blk, lambda i: (i, 1)),
                  pl.BlockSpec(blk, lambda i: (i, 2)),
                  far_blk, far_blk, near_blk, near_blk,
                  pl.BlockSpec(cnt_far.shape, lambda i: (0, 0)),
                  pl.BlockSpec(cnt_near.shape, lambda i: (0, 0))],
        out_specs=pl.BlockSpec(blk, lambda i: (i, 0)),
        out_shape=jax.ShapeDtypeStruct((bt, d_attn), F32),
        compiler_params=_cparams("arbitrary"),
        name="attn_sample",
    )(p, p, p, cache_k.reshape(far_shape), cache_v.reshape(far_shape),
      cache_k.reshape(near_shape), cache_v.reshape(near_shape), cnt_far, cnt_near)


def _causal_conv_silu(x, prev8, cw):
    t = x.shape[0]
    row8 = lax.broadcasted_iota(jnp.int32, (SUBLANES, 1), 0)
    conv = x * cw[CONV_W - 1:CONV_W, :]
    for i in range(1, CONV_W):
        xr = pltpu.roll(x, i, 0)
        pr = pltpu.roll(prev8, i, 0)
        head = jnp.where(row8 < i, pr, xr[:SUBLANES, :])
        xs = head if t == SUBLANES else jnp.concatenate([head, xr[SUBLANES:, :]], axis=0)
        conv = conv + xs * cw[CONV_W - 1 - i:CONV_W - i, :]
    return conv * _sigmoid(conv)


def _l2norm(x):
    return x * lax.rsqrt(jnp.sum(x * x, axis=-1, keepdims=True) + L2_EPS)


def _gated_rmsnorm(o, g_row, z):
    y = o * lax.rsqrt(jnp.mean(o * o, axis=-1, keepdims=True) + NORM_EPS) * g_row
    return y * (z * _sigmoid(z))


def _beta_and_g(ba, alog_row, dtb_row):
    beta = _sigmoid(ba)
    g = -jnp.exp(alog_row) * _softplus(ba + dtb_row)
    return beta, g


def _gdn_prompt_kernel(qkv_ref, z_ref, ba_ref, cw_ref, alog_ref, dtb_ref, gn_ref,
                       o_ref, tail_ref, ssm_ref, s_scr, prev_scr, *, n_heads):
    n = pl.program_id(0)
    c = GDN_CHUNK
    hd = HEAD_DIM

    @pl.when(n == 0)
    def _():
        s_scr[...] = jnp.zeros_like(s_scr)
        prev_scr[...] = jnp.zeros_like(prev_scr)

    x = qkv_ref[...]
    conv = _causal_conv_silu(x, prev_scr[...], cw_ref[...])
    prev_scr[...] = x[c - SUBLANES:, :]
    tail_ref[...] = x[c - SUBLANES:, :]

    beta_all, g_all = _beta_and_g(ba_ref[...], alog_ref[...], dtb_ref[...])
    ri = lax.broadcasted_iota(jnp.int32, (c, c), 0)
    ci = lax.broadcasted_iota(jnp.int32, (c, c), 1)
    tri_incl = ri >= ci
    tri_strict = ri > ci
    eye = jnp.where(ri == ci, 1.0, 0.0).astype(F32)
    blk_xor = ri ^ ci
    tri_b = jnp.where(tri_incl, 1.0, 0.0).astype(BF16)
    g_hi, g_mid, g_lo = _split3(g_all)
    gcum = _dot(tri_b, g_hi) + _dot(tri_b, g_mid) + _dot(tri_b, g_lo)
    gcum_t = gcum.T
    gn = gn_ref[...]

    hs = range(n_heads)
    q = [_l2norm(conv[:, h * hd:(h + 1) * hd]) * (hd ** -0.5) for h in hs]
    k = [_l2norm(conv[:, (n_heads + h) * hd:(n_heads + h + 1) * hd]) for h in hs]
    v = [conv[:, (2 * n_heads + h) * hd:(2 * n_heads + h + 1) * hd] for h in hs]
    beta = [beta_all[:, h:h + 1] for h in hs]
    gc = [gcum[:, n_heads + h:n_heads + h + 1] for h in hs]
    gr = [gcum_t[n_heads + h:n_heads + h + 1, :] for h in hs]
    g_last = [gcum[c - 1:c, n_heads + h:n_heads + h + 1] for h in hs]
    decay = [jnp.where(tri_incl, jnp.exp(gc[h] - gr[h]), 0.0) for h in hs]
    kb = [k[h] * beta[h] for h in hs]
    k16 = [k[h].astype(BF16) for h in hs]
    a = [jnp.where(tri_strict, _dot_nt(kb[h].astype(BF16), k16[h]) * decay[h], 0.0) for h in hs]
    qk = [_dot_nt(q[h].astype(BF16), k16[h]) * decay[h] for h in hs]
    t_inv = [eye - jnp.where((blk_xor >> 1) == 0, a[h], 0.0) for h in hs]
    for lvl in range(2, int(math.log2(c)) + 1):
        join = (blk_xor >> (lvl - 1)) == 1
        t16 = [t_inv[h].astype(BF16) for h in hs]
        lt = [_dot(jnp.where(join, a[h], 0.0).astype(BF16), t16[h]).astype(BF16) for h in hs]
        t_inv = [t_inv[h] - _dot(t16[h], lt[h]) for h in hs]
    rhs = [jnp.concatenate([v[h] * beta[h], kb[h] * jnp.exp(gc[h])], axis=-1).astype(BF16) for h in hs]
    sol = [_dot(t_inv[h].astype(BF16), rhs[h]) for h in hs]
    s_old = [s_scr[h] for h in hs]
    s16 = [s_old[h].astype(BF16) for h in hs]
    v_new = [sol[h][:, :hd] - _dot(sol[h][:, hd:].astype(BF16), s16[h]) for h in hs]
    v16 = [v_new[h].astype(BF16) for h in hs]
    o = [_dot((q[h] * jnp.exp(gc[h])).astype(BF16), s16[h]) + _dot(qk[h].astype(BF16), v16[h]) for h in hs]
    k_dec = [(k[h] * jnp.exp(g_last[h] - gc[h])).T.astype(BF16) for h in hs]
    for h in hs:
        s_scr[h] = s_old[h] * jnp.exp(g_last[h]) + _dot(k_dec[h], v16[h])
        o_ref[:, h * hd:(h + 1) * hd] = _gated_rmsnorm(o[h], gn, z_ref[:, h * hd:(h + 1) * hd])

    @pl.when(n == pl.num_programs(0) - 1)
    def _():
        ssm_ref[...] = s_scr[...]


def gdn_prompt(p, pba, conv_w, alog_row, dtb_row, g_norm, *, n_heads, qkv_col_block, z_col_block):
    s = p.shape[0]
    c = GDN_CHUNK
    hd = HEAD_DIM
    conv_dim = 3 * n_heads * hd
    kern = functools.partial(_gdn_prompt_kernel, n_heads=n_heads)
    return pl.pallas_call(
        kern,
        grid=(s // c,),
        in_specs=[pl.BlockSpec((c, conv_dim), lambda n: (n, qkv_col_block)),
                  pl.BlockSpec((c, n_heads * hd), lambda n: (n, z_col_block)),
                  pl.BlockSpec((c, LANES), lambda n: (n, 0)),
                  pl.BlockSpec((CONV_W, conv_dim), lambda n: (0, 0)),
                  pl.BlockSpec((1, LANES), lambda n: (0, 0)),
                  pl.BlockSpec((1, LANES), lambda n: (0, 0)),
                  pl.BlockSpec((1, hd), lambda n: (0, 0))],
        out_specs=[pl.BlockSpec((c, n_heads * hd), lambda n: (n, 0)),
                   pl.BlockSpec((SUBLANES, conv_dim), lambda n: (0, 0)),
                   pl.BlockSpec((n_heads, hd, hd), lambda n: (0, 0, 0))],
        out_shape=[jax.ShapeDtypeStruct((s, n_heads * hd), F32),
                   jax.ShapeDtypeStruct((SUBLANES, conv_dim), F32),
                   jax.ShapeDtypeStruct((n_heads, hd, hd), F32)],
        scratch_shapes=[pltpu.VMEM((n_heads, hd, hd), F32), pltpu.VMEM((SUBLANES, conv_dim), F32)],
        compiler_params=_cparams("arbitrary"),
        name="gdn_prompt",
    )(p, p, pba, conv_w, alog_row, dtb_row, g_norm.reshape(1, hd))


def _gdn_sample_kernel(qkv_ref, z_ref, ba_ref, prev_ref, s0_ref, cw_ref, alog_ref, dtb_ref, gn_ref,
                       o_ref, ssm_ref, *, n_heads, t_new, n_seq):
    hd = HEAD_DIM
    cw = cw_ref[...]
    gn = gn_ref[...]
    beta_all, g_all = _beta_and_g(ba_ref[...], alog_ref[...], dtb_ref[...])
    eg_all = jnp.exp(g_all)
    pad = jnp.zeros((hd - 2 * t_new, hd), F32)
    chains = [(bi, h) for bi in range(n_seq) for h in range(n_heads)]
    conv = [_causal_conv_silu(qkv_ref[bi * t_new:(bi + 1) * t_new, :], prev_ref[bi], cw) for bi in range(n_seq)]
    q = {(bi, h): _l2norm(conv[bi][:, h * hd:(h + 1) * hd]) * (hd ** -0.5) for bi, h in chains}
    k = {(bi, h): _l2norm(conv[bi][:, (n_heads + h) * hd:(n_heads + h + 1) * hd]) for bi, h in chains}
    v = {(bi, h): conv[bi][:, (2 * n_heads + h) * hd:(2 * n_heads + h + 1) * hd] for bi, h in chains}
    kq_t = {c: jnp.concatenate([k[c], q[c], pad], axis=0).T for c in chains}
    s = {(bi, h): s0_ref[bi, h] for bi, h in chains}
    rows = {c: [] for c in chains}
    for t in range(t_new):
        for bi, h in chains:
            c = (bi, h)
            r = bi * t_new + t
            kb = jnp.broadcast_to(kq_t[c][:, t:t + 1], (hd, hd))
            qb = jnp.broadcast_to(kq_t[c][:, t_new + t:t_new + t + 1], (hd, hd))
            eg = eg_all[r:r + 1, n_heads + h:n_heads + h + 1]
            bt = beta_all[r:r + 1, h:h + 1]
            sd = s[c] * eg
            ks = jnp.sum(sd * kb, axis=0, keepdims=True)
            v_new = bt * (v[c][t:t + 1, :] - ks)
            s[c] = sd + kb * v_new
            rows[c].append(jnp.sum(s[c] * qb, axis=0, keepdims=True))
    for bi, h in chains:
        ssm_ref[bi, h] = s[(bi, h)]
        o = jnp.concatenate(rows[(bi, h)], axis=0)
        rs = slice(bi * t_new, (bi + 1) * t_new)
        o_ref[rs, h * hd:(h + 1) * hd] = _gated_rmsnorm(o, gn, z_ref[rs, h * hd:(h + 1) * hd])


def gdn_sample(p, pba, prev8, s0, conv_w, alog_row, dtb_row, g_norm, *, n_heads, t_new, qkv_col_block, z_col_block):
    bt = p.shape[0]
    b = bt // t_new
    hd = HEAD_DIM
    conv_dim = 3 * n_heads * hd
    n_seq = math.gcd(GDN_SAMPLE_SEQS, b)
    rows = n_seq * t_new
    kern = functools.partial(_gdn_sample_kernel, n_heads=n_heads, t_new=t_new, n_seq=n_seq)
    return pl.pallas_call(
        kern,
        grid=(b // n_seq,),
        in_specs=[pl.BlockSpec((rows, conv_dim), lambda i: (i, qkv_col_block)),
                  pl.BlockSpec((rows, n_heads * hd), lambda i: (i, z_col_block)),
                  pl.BlockSpec((rows, LANES), lambda i: (i, 0)),
                  pl.BlockSpec((n_seq, SUBLANES, conv_dim), lambda i: (i, 0, 0)),
                  pl.BlockSpec((n_seq, n_heads, hd, hd), lambda i: (i, 0, 0, 0)),
                  pl.BlockSpec((CONV_W, conv_dim), lambda i: (0, 0)),
                  pl.BlockSpec((1, LANES), lambda i: (0, 0)),
                  pl.BlockSpec((1, LANES), lambda i: (0, 0)),
                  pl.BlockSpec((1, hd), lambda i: (0, 0))],
        out_specs=[pl.BlockSpec((rows, n_heads * hd), lambda i: (i, 0)),
                   pl.BlockSpec((n_seq, n_heads, hd, hd), lambda i: (i, 0, 0, 0))],
        out_shape=[jax.ShapeDtypeStruct((bt, n_heads * hd), F32),
                   jax.ShapeDtypeStruct((b, n_heads, hd, hd), F32)],
        compiler_params=_cparams("arbitrary"),
        name="gdn_sample",
    )(p, p, pba, prev8, s0, conv_w, alog_row, dtb_row, g_norm.reshape(1, hd))


def _outproj_kernel(oa_ref, ob_ref, x_ref, gate_ref, sc_ref, sh_ref, gpost_ref, gpre_ref, wo_ref, wr_hi_ref, wr_lo_ref,
                    br_ref, x1_ref, h2_ref, tw_ref, ti_ref):
    d_attn = oa_ref.shape[1]
    mix = (_dot(oa_ref[...].astype(BF16), wo_ref[:d_attn, :]) + _dot(ob_ref[...].astype(BF16), wo_ref[d_attn:, :]))
    y = mix * lax.rsqrt(jnp.mean(mix * mix, axis=-1, keepdims=True) + NORM_EPS) * gpost_ref[...]
    x1 = x_ref[...] + gate_ref[...] * y
    x1_ref[...] = x1
    hn = x1 * lax.rsqrt(jnp.mean(x1 * x1, axis=-1, keepdims=True) + NORM_EPS) * gpre_ref[...]
    h2 = hn * (1.0 + sc_ref[...]) + sh_ref[...]
    h2_ref[...] = _pack_bf16_pairs(h2)

    h_hi = h2.astype(BF16)
    h_lo = (h2 - h_hi.astype(F32)).astype(BF16)
    logits = (_dot(h_hi, wr_hi_ref[...]) + _dot(h_hi, wr_lo_ref[...]) + _dot(h_lo, wr_hi_ref[...])) + br_ref[...]
    lane = lax.broadcasted_iota(jnp.int32, logits.shape, 1)
    tops, idxs = [], []
    for _ in range(TOP_K):
        mx = jnp.max(logits, axis=-1, keepdims=True)
        ix = jnp.min(jnp.where(logits == mx, lane, LANES), axis=-1, keepdims=True)
        tops.append(mx)
        idxs.append(ix)
        logits = jnp.where(lane == ix, 2.0 * NEG_BIG, logits)
    es = [jnp.exp(t - tops[0]) for t in tops]
    tot = es[0]
    for e in es[1:]:
        tot = tot + e
    tw = jnp.zeros(logits.shape, F32)
    ti = jnp.zeros(logits.shape, jnp.int32)
    for kk in range(TOP_K):
        tw = jnp.where(lane == kk, es[kk] / tot, tw)
        ti = jnp.where(lane == kk, idxs[kk], ti)
    tw_ref[...] = tw
    ti_ref[...] = ti


def out_proj(o_a, o_b, x, gate, scale, shift, g_post, g_pre, wo16, wr_hi, wr_lo, br_row, *, tm):
    rows, d = x.shape
    tm = min(tm, rows)
    d_attn = o_a.shape[1]
    mod_rows = gate.shape[0]
    mod_block = (tm, d) if mod_rows == rows else (1, d)
    mod_map = (lambda i: (i, 0)) if mod_rows == rows else (lambda i: (0, 0))
    half = pl.BlockSpec((tm, d_attn), lambda i: (i, 0))
    full = pl.BlockSpec((tm, d), lambda i: (i, 0))
    row_d = pl.BlockSpec((1, d), lambda i: (0, 0))
    lane_blk = pl.BlockSpec((tm, LANES), lambda i: (i, 0))
    in_specs = [half, pl.BlockSpec((tm, d - d_attn), lambda i: (i, 0)), full,
                pl.BlockSpec(mod_block, mod_map), pl.BlockSpec(mod_block, mod_map), pl.BlockSpec(mod_block, mod_map),
                row_d, row_d,
                pl.BlockSpec((d, d), lambda i: (0, 0)),
                pl.BlockSpec((d, LANES), lambda i: (0, 0)), pl.BlockSpec((d, LANES), lambda i: (0, 0)),
                pl.BlockSpec((1, LANES), lambda i: (0, 0))]
    args = [o_a, o_b, x, gate, scale, shift, g_post.reshape(1, d), g_pre.reshape(1, d), wo16, wr_hi, wr_lo, br_row]
    return pl.pallas_call(
        _outproj_kernel,
        grid=(rows // tm,),
        in_specs=in_specs,
        out_specs=[full, pl.BlockSpec((tm, d // 2), lambda i: (i, 0)), lane_blk, lane_blk],
        out_shape=[jax.ShapeDtypeStruct((rows, d), F32), jax.ShapeDtypeStruct((rows, d // 2), jnp.uint32),
                   jax.ShapeDtypeStruct((rows, LANES), F32), jax.ShapeDtypeStruct((rows, LANES), jnp.int32)],
        compiler_params=_cparams("arbitrary"),
        name="out_proj",
    )(*args)


def _pack_bf16_pairs(x):
    half = x.shape[1] // 2
    bits = pltpu.bitcast(x.astype(BF16).astype(F32), jnp.uint32)
    return bits[:, :half] | (bits[:, half:] >> 16)


def _unpack_bf16_pairs(w):
    hi = pltpu.bitcast(w & jnp.uint32(0xFFFF0000), F32).astype(BF16)
    lo = pltpu.bitcast(w << 16, F32).astype(BF16)
    return hi, lo


DISPATCH_TOKENS = 256
ROW_DMA_UNROLL = 8


def _dispatch_kernel(pos_ref, zs_ref, h_ref, x_hbm, zbuf, sem_z, sem):
    i = pl.program_id(0)
    tt = h_ref.shape[0]
    sb = zbuf.shape[0]
    n_items = zs_ref.shape[0]
    n_sub = MOE_BLOCK_ROWS // sb

    def zero_copy(w, s):
        row0 = pl.multiple_of((w * n_sub + s) * sb, sb)
        return pltpu.make_async_copy(zbuf, x_hbm.at[pl.ds(row0, sb), :], sem_z)

    @pl.when(i == 0)
    def _():
        zbuf[...] = jnp.zeros(zbuf.shape, zbuf.dtype)

        def start(w, carry):
            lax.fori_loop(zs_ref[w], n_sub, lambda s, c: (zero_copy(w, s).start(), c)[1], 0)
            return carry
        lax.fori_loop(0, n_items, start, 0)

        def wait(w, carry):
            lax.fori_loop(zs_ref[w], n_sub, lambda s, c: (zero_copy(w, s).wait(), c)[1], 0)
            return carry
        lax.fori_loop(0, n_items, wait, 0)

    base = i * (tt * TOP_K)

    def row_copy(t, kk):
        dst = pos_ref[base + t * TOP_K + kk]
        return pltpu.make_async_copy(h_ref.at[pl.ds(t, 1), :], x_hbm.at[pl.ds(dst, 1), :], sem)

    for kk in range(TOP_K):
        def start(t, carry, kk=kk):
            row_copy(t, kk).start()
            return carry
        lax.fori_loop(0, tt, start, 0, unroll=ROW_DMA_UNROLL)
    for kk in range(TOP_K):
        def wait(t, carry, kk=kk):
            row_copy(t, kk).wait()
            return carry
        lax.fori_loop(0, tt, wait, 0, unroll=ROW_DMA_UNROLL)


def moe_dispatch(h2p, pos_flat, zero_from):
    n_tok, half = h2p.shape
    tt = math.gcd(DISPATCH_TOKENS, n_tok)
    assert tt % SUBLANES == 0
    n_items = zero_from.shape[0]
    grid_spec = pltpu.PrefetchScalarGridSpec(
        num_scalar_prefetch=2,
        grid=(n_tok // tt,),
        in_specs=[pl.BlockSpec((tt, half), lambda i, pos, zs: (i, 0))],
        out_specs=pl.BlockSpec(memory_space=pl.ANY),
        scratch_shapes=[pltpu.VMEM((MOE_SUB_ROWS, half), jnp.uint32), pltpu.SemaphoreType.DMA(()),
                        pltpu.SemaphoreType.DMA(())],
    )
    return pl.pallas_call(
        _dispatch_kernel,
        grid_spec=grid_spec,
        out_shape=jax.ShapeDtypeStruct((n_items * MOE_BLOCK_ROWS, half), jnp.uint32),
        compiler_params=_cparams("arbitrary"),
        name="moe_dispatch",
    )(pos_flat, zero_from, h2p)


def _moe_kernel(e_ref, ns_ref, x_ref, wg_ref, wu_ref, bg_ref, bu_ref, wd_ref, bd_ref, o_ref, xb_scr):
    w = pl.program_id(0)
    j = pl.program_id(1)
    ns = ns_ref[w]
    sb = MOE_SUB_ROWS
    n_sub = MOE_BLOCK_ROWS // sb
    half = x_ref.shape[1]

    @pl.when(j == 0)
    def _():
        def zero(s, carry):
            r0 = pl.multiple_of(s * sb, sb)
            o_ref[pl.ds(r0, sb), :] = jnp.zeros((sb, o_ref.shape[1]), F32)
            return carry
        lax.fori_loop(ns, n_sub, zero, 0)

        def unpack(s, carry):
            rs = pl.ds(pl.multiple_of(s * sb, sb), sb)
            hi, lo = _unpack_bf16_pairs(x_ref[rs, :])
            xb_scr[rs, :half] = hi
            xb_scr[rs, half:] = lo
            return carry
        lax.fori_loop(0, ns, unpack, 0)

    @pl.when(ns > 0)
    def _():
        wg = wg_ref[0].astype(BF16)
        wu = wu_ref[0].astype(BF16)
        wd = wd_ref[0].astype(BF16)
        bg = bg_ref[0]
        bu = bu_ref[0]
        bd = bd_ref[0]

        def compute(r0, n_rows):
            rs = pl.ds(r0, n_rows)
            xs = xb_scr[rs, :]
            gate = jnp.minimum(_dot(xs, wg) + bg, SWIGLU_LIMIT)
            up = jnp.clip(_dot(xs, wu) + bu, -SWIGLU_LIMIT, SWIGLU_LIMIT)
            act = (gate * _sigmoid(SWIGLU_ALPHA * gate) * (up + 1.0)).astype(BF16)
            dc = MOE_DOWN_CHUNK
            for c0 in range(0, o_ref.shape[1], dc):
                y = _dot(act, wd[:, c0:c0 + dc])

                @pl.when(j == 0)
                def _():
                    o_ref[rs, c0:c0 + dc] = y + bd[:, c0:c0 + dc]

                @pl.when(j > 0)
                def _():
                    o_ref[rs, c0:c0 + dc] += y

        @pl.when(ns == n_sub)
        def _():
            compute(0, MOE_BLOCK_ROWS)

        @pl.when(ns < n_sub)
        def _():
            size = MOE_BLOCK_ROWS // 2
            while size >= sb:
                bit = (ns & (size // sb)) != 0
                off = (ns & ~(2 * (size // sb) - 1)) * sb
                pl.when(bit)(functools.partial(compute, pl.multiple_of(off, size), size))
                size //= 2


def moe_experts(x_sorted, item_e, item_ns, w_gate_up, b_gate_up, w_down, b_down):
    n_rows, half = x_sorted.shape
    d = 2 * half
    n_e, _, two_ff = w_gate_up.shape
    d_ff = two_ff // 2
    bm = MOE_BLOCK_ROWS
    tf = MOE_FF_TILE
    n_f = d_ff // tf
    n_items = n_rows // bm

    def jj(j, ns, w):
        return jnp.where(ns[w] > 0, j, n_f - 1)

    grid_spec = pltpu.PrefetchScalarGridSpec(
        num_scalar_prefetch=2,
        grid=(n_items, n_f),
        in_specs=[pl.BlockSpec((bm, half), lambda w, j, e, ns: (w, 0)),
                  pl.BlockSpec((1, d, tf), lambda w, j, e, ns: (e[w], 0, jj(j, ns, w))),
                  pl.BlockSpec((1, d, tf), lambda w, j, e, ns: (e[w], 0, n_f + jj(j, ns, w))),
                  pl.BlockSpec((1, 1, tf), lambda w, j, e, ns: (e[w], 0, jj(j, ns, w))),
                  pl.BlockSpec((1, 1, tf), lambda w, j, e, ns: (e[w], 0, n_f + jj(j, ns, w))),
                  pl.BlockSpec((1, tf, d), lambda w, j, e, ns: (e[w], jj(j, ns, w), 0)),
                  pl.BlockSpec((1, 1, d), lambda w, j, e, ns: (e[w], 0, 0))],
        out_specs=pl.BlockSpec((bm, d), lambda w, j, e, ns: (w, 0)),
        scratch_shapes=[pltpu.VMEM((bm, d), BF16)],
    )
    return pl.pallas_call(
        _moe_kernel,
        grid_spec=grid_spec,
        out_shape=jax.ShapeDtypeStruct((n_rows, d), F32),
        compiler_params=_cparams("arbitrary", "arbitrary"),
        name="moe_experts",
    )(item_e, item_ns, x_sorted, w_gate_up, w_gate_up, b_gate_up.reshape(n_e, 1, two_ff),
      b_gate_up.reshape(n_e, 1, two_ff), w_down, b_down.reshape(n_e, 1, d))


def moe_routing(top_e):
    n_tok = top_e.shape[0]
    n_asg = n_tok * TOP_K
    bm = MOE_BLOCK_ROWS
    sb = MOE_SUB_ROWS
    n_items = n_asg // bm + N_EXPERTS
    i32 = jnp.int32
    onehot = jnp.sum((top_e[:, :, None] == jnp.arange(N_EXPERTS, dtype=i32)[None, None, :]).astype(i32), axis=1)
    counts = jnp.sum(onehot, axis=0)
    earlier = jnp.cumsum(onehot, axis=0) - onehot
    rank = jnp.take_along_axis(earlier, top_e, axis=1)
    n_it = (counts + bm - 1) // bm
    it_end = jnp.cumsum(n_it)
    it_base = it_end - n_it
    pos = (it_base[top_e] * bm + rank).astype(i32)
    w = jnp.arange(n_items, dtype=i32)
    total = it_end[-1]
    e_raw = jnp.minimum(jnp.sum((it_end[None, :] <= w[:, None]).astype(i32), axis=1), N_EXPERTS - 1)
    rows = jnp.clip(counts[e_raw] - (w - it_base[e_raw]) * bm, 0, bm)
    rows = jnp.where(w < total, rows, 0)
    item_ns = (rows + sb - 1) // sb
    item_e = jnp.where(w < total, e_raw, e_raw[jnp.maximum(total - 1, 0)])
    zero_from = jnp.maximum(item_ns - 1, 0)
    return pos, item_e.astype(i32), item_ns.astype(i32), zero_from.astype(i32)


def _finish_kernel(pos_ref, x1_ref, tw_ref, gate_ref, g_ref, y_hbm, o_ref, ybuf, sem):
    i = pl.program_id(0)
    n_tiles = pl.num_programs(0)
    tm, d = x1_ref.shape

    def row_copy(base, slot, kk, t):
        src = pos_ref[base + t * TOP_K + kk]
        return pltpu.make_async_copy(y_hbm.at[pl.ds(src, 1), :], ybuf.at[slot, kk, pl.ds(t, 1), :], sem.at[slot])

    def issue(tile, slot):
        base = tile * (tm * TOP_K)
        for kk in range(TOP_K):
            def body(t, carry, kk=kk):
                row_copy(base, slot, kk, t).start()
                return carry
            lax.fori_loop(0, tm, body, 0, unroll=ROW_DMA_UNROLL)

    def wait(tile, slot):
        base = tile * (tm * TOP_K)
        for kk in range(TOP_K):
            def body(t, carry, kk=kk):
                row_copy(base, slot, kk, t).wait()
                return carry
            lax.fori_loop(0, tm, body, 0, unroll=ROW_DMA_UNROLL)

    @pl.when(i == 0)
    def _():
        issue(0, 0)

    @pl.when(i + 1 < n_tiles)
    def _():
        issue(i + 1, (i + 1) % 2)

    slot = i % 2
    wait(i, slot)
    tw = tw_ref[...]
    ff = tw[:, 0:1] * ybuf[slot, 0]
    for kk in range(1, TOP_K):
        ff = ff + tw[:, kk:kk + 1] * ybuf[slot, kk]
    y = ff * lax.rsqrt(jnp.mean(ff * ff, axis=-1, keepdims=True) + NORM_EPS) * g_ref[...]
    o_ref[...] = x1_ref[...] + gate_ref[...] * y


def finish(x1, y_sorted, pos, tw, gate, g_post, *, tm):
    rows, d = x1.shape
    tm = min(tm, rows)
    mod_rows = gate.shape[0]
    mod_block = (tm, d) if mod_rows == rows else (1, d)
    mod_map = (lambda i, pos: (i, 0)) if mod_rows == rows else (lambda i, pos: (0, 0))
    grid_spec = pltpu.PrefetchScalarGridSpec(
        num_scalar_prefetch=1,
        grid=(rows // tm,),
        in_specs=[pl.BlockSpec((tm, d), lambda i, pos: (i, 0)),
                  pl.BlockSpec((tm, LANES), lambda i, pos: (i, 0)),
                  pl.BlockSpec(mod_block, mod_map),
                  pl.BlockSpec((1, d), lambda i, pos: (0, 0)),
                  pl.BlockSpec(memory_space=pl.ANY)],
        out_specs=pl.BlockSpec((tm, d), lambda i, pos: (i, 0)),
        scratch_shapes=[pltpu.VMEM((2, TOP_K, tm, d), F32), pltpu.SemaphoreType.DMA((2,))],
    )
    return pl.pallas_call(
        _finish_kernel,
        grid_spec=grid_spec,
        out_shape=jax.ShapeDtypeStruct((rows, d), F32),
        compiler_params=_cparams("arbitrary"),
        name="finish",
    )(pos, x1, tw, gate, g_post.reshape(1, d), y_sorted)


def _rope_tables(pos):
    half = HEAD_DIM // 2
    inv_freq = ROPE_THETA ** (-np.arange(half, dtype=np.float64) / half)
    ang = np.asarray(pos, np.float64)[:, None] * inv_freq[None, :]
    cos = np.cos(ang)
    sin = np.sin(ang)
    return (jnp.asarray(np.concatenate([cos, cos], axis=-1), F32),
            jnp.asarray(np.concatenate([-sin, sin], axis=-1), F32))


def _lane_row(vals, offset):
    return jnp.zeros((1, LANES), F32).at[0, offset:offset + vals.shape[0]].set(vals.astype(F32))


def kernel(x_prompt, x_sample, c_prompt, c_sample, cache_win_k, cache_win_v, state_conv, state_ssm, w_ada, b_ada, g_pre_mix, g_post_mix, g_pre_ffn, g_post_ffn, w_in, conv_w, a_log, dt_bias, g_gdn_norm, w_out, w_router, b_router, w_gate_up, b_gate_up, w_down, b_down):
    depth = w_ada.shape[0]
    assert depth == 1 and x_prompt.shape[0] == 1
    _, s_len, d = x_prompt.shape
    b_s, t_s, _ = x_sample.shape
    n_heads_a = cache_win_k.shape[3]
    d_attn = n_heads_a * HEAD_DIM
    n_heads_b = state_ssm.shape[2]
    conv_dim = 3 * n_heads_b * HEAD_DIM
    n_samp = b_s * t_s
    win_p = min(max(w for w, _ in DILATED_PATTERNS), s_len)

    c_all = jnp.concatenate([c_prompt, c_sample], axis=0)
    c_rows = -(-c_all.shape[0] // SUBLANES) * SUBLANES
    c_all = jnp.pad(c_all, ((0, c_rows - c_all.shape[0]), (0, 0)))
    mod = ada_mod(c_all, w_ada[0], b_ada[0])
    mod_p = [mod[0:1, i * d:(i + 1) * d] for i in range(6)]
    mod_s = [jnp.repeat(mod[1:1 + b_s, i * d:(i + 1) * d], t_s, axis=0) for i in range(6)]

    n_main = d_attn * 3 + conv_dim + n_heads_b * HEAD_DIM
    w_ba = jnp.pad(w_in[0][:, n_main:], ((0, 0), (0, LANES - (w_in.shape[2] - n_main))))
    cos_p, sin_p = _rope_tables(np.arange(s_len))
    cos_s, sin_s = _rope_tables(PAST_LEN + (np.arange(n_samp) % t_s))
    xp = x_prompt[0]
    xs = x_sample.reshape(n_samp, d)
    p_p, pba_p = in_proj(xp, mod_p[1], mod_p[0], g_pre_mix[0], cos_p, sin_p, w_in[0], w_ba, tm=1024, d_attn=d_attn)
    p_s, pba_s = in_proj(xs, mod_s[1], mod_s[0], g_pre_mix[0], cos_s, sin_s, w_in[0], w_ba, tm=512, d_attn=d_attn)

    oa_p = attn_prompt(p_p, n_heads=n_heads_a)
    oa_s = attn_sample(p_s, cache_win_k[0], cache_win_v[0], t_new=t_s)

    alog_row = _lane_row(a_log[0], n_heads_b)
    dtb_row = _lane_row(dt_bias[0], n_heads_b)
    qkv_blk = (3 * d_attn) // conv_dim
    z_blk = (3 * d_attn + conv_dim) // (n_heads_b * HEAD_DIM)
    assert qkv_blk * conv_dim == 3 * d_attn and z_blk * n_heads_b * HEAD_DIM == 3 * d_attn + conv_dim
    ob_p, tail_p, ssm_p = gdn_prompt(p_p, pba_p, conv_w[0], alog_row, dtb_row, g_gdn_norm[0],
                                     n_heads=n_heads_b, qkv_col_block=qkv_blk, z_col_block=z_blk)
    prev8 = jnp.pad(state_conv[0], ((0, 0), (SUBLANES - (CONV_W - 1), 0), (0, 0)))
    ob_s, ssm_s = gdn_sample(p_s, pba_s, prev8, state_ssm[0], conv_w[0], alog_row, dtb_row, g_gdn_norm[0],
                             n_heads=n_heads_b, t_new=t_s, qkv_col_block=qkv_blk, z_col_block=z_blk)

    wo16 = w_out[0].astype(BF16)
    wr = jnp.pad(w_router[0], ((0, 0), (0, LANES - N_EXPERTS)))
    wr_hi = wr.astype(BF16)
    wr_lo = (wr - wr_hi.astype(F32)).astype(BF16)
    br_row = jnp.full((1, LANES), NEG_BIG, F32).at[0, :N_EXPERTS].set(b_router[0])
    x1_p, h2_p, tw_p, ti_p = out_proj(oa_p, ob_p, xp, mod_p[2], mod_p[4], mod_p[3], g_post_mix[0], g_pre_ffn[0],
                                      wo16, wr_hi, wr_lo, br_row, tm=256)
    x1_s, h2_s, tw_s, ti_s = out_proj(oa_s, ob_s, xs, mod_s[2], mod_s[4], mod_s[3], g_post_mix[0], g_pre_ffn[0],
                                      wo16, wr_hi, wr_lo, br_row, tm=256)

    h2 = jnp.concatenate([h2_p, h2_s], axis=0)
    top_e = jnp.concatenate([ti_p[:, :TOP_K], ti_s[:, :TOP_K]], axis=0)
    pos, item_e, item_ns, zero_from = moe_routing(top_e)
    x_sorted = moe_dispatch(h2, pos.reshape(-1), zero_from)
    y_sorted = moe_experts(x_sorted, item_e, item_ns, w_gate_up[0], b_gate_up[0], w_down[0], b_down[0])
    y_p = finish(x1_p, y_sorted, pos[:s_len].reshape(-1), tw_p, mod_p[5], g_post_ffn[0], tm=128)
    y_s = finish(x1_s, y_sorted, pos[s_len:].reshape(-1), tw_s, mod_s[5], g_post_ffn[0], tm=128)

    hd = HEAD_DIM
    k_p = p_p[s_len - win_p:, d_attn:2 * d_attn].reshape(1, 1, win_p, n_heads_a, hd)
    v_p = p_p[s_len - win_p:, 2 * d_attn:3 * d_attn].reshape(1, 1, win_p, n_heads_a, hd)
    conv_p = tail_p[SUBLANES - (CONV_W - 1):].reshape(1, 1, CONV_W - 1, conv_dim)
    k_s = p_s[:, d_attn:2 * d_attn].reshape(1, b_s, t_s, n_heads_a, hd)
    v_s = p_s[:, 2 * d_attn:3 * d_attn].reshape(1, b_s, t_s, n_heads_a, hd)
    conv_s = p_s.reshape(b_s, t_s, -1)[:, t_s - (CONV_W - 1):, 3 * d_attn:3 * d_attn + conv_dim][None]
    return (y_p[None], y_s.reshape(b_s, t_s, d), k_p, v_p, conv_p, ssm_p[None, None],
            k_s, v_s, conv_s, ssm_s[None])
```

```python
import functools
import math

import numpy as np
import jax
import jax.numpy as jnp
from jax import lax
from jax.experimental import pallas as pl
from jax.experimental.pallas import tpu as pltpu

F32 = jnp.float32
BF16 = jnp.bfloat16

HEAD_DIM = 128
DILATED_PATTERNS = ((128, 1), (512, 4), (2048, 16))
ATTN_BLOCK = 128
PAST_LEN = 2048
ROPE_THETA = 10000.0
CONV_W = 4
N_EXPERTS = 32
TOP_K = 4
SWIGLU_LIMIT = 7.0
SWIGLU_ALPHA = 1.702
NORM_EPS = 1e-6
L2_EPS = 1e-6

LANES = 128
SUBLANES = 8
VMEM_LIMIT_BYTES = 56 * 1024 * 1024

NEG_BIG = -1e30

GDN_CHUNK = 128
GDN_SAMPLE_SEQS = 2
MOE_BLOCK_ROWS = 1024
MOE_SUB_ROWS = 128
MOE_FF_TILE = 256
MOE_DOWN_CHUNK = 512


def _cparams(*sem):
    return pltpu.CompilerParams(dimension_semantics=sem, vmem_limit_bytes=VMEM_LIMIT_BYTES)


def _sigmoid(x):
    return 1.0 / (1.0 + jnp.exp(-x))


def _softplus(x):
    return jnp.maximum(x, 0.0) + jnp.log(1.0 + jnp.exp(-jnp.abs(x)))


def _dot(a, b):
    return jnp.dot(a, b, preferred_element_type=F32)


def _dot_nt(a, b):
    return lax.dot_general(a, b, (((1,), (1,)), ((), ())), preferred_element_type=F32)


def _split3(x):
    hi = x.astype(BF16)
    r = x - hi.astype(F32)
    mid = r.astype(BF16)
    lo = (r - mid.astype(F32)).astype(BF16)
    return hi, mid, lo


def _ada_kernel(c_ref, w_ref, b_ref, o_ref):
    c = c_ref[...]
    a = (c * _sigmoid(c)).astype(BF16)
    o_ref[...] = _dot(a, w_ref[...].astype(BF16)) + b_ref[...]


def ada_mod(c, w_ada, b_ada):
    rows, d = c.shape
    n = w_ada.shape[1]
    tn = 1024
    return pl.pallas_call(
        _ada_kernel,
        grid=(n // tn,),
        in_specs=[pl.BlockSpec((rows, d), lambda j: (0, 0)),
                  pl.BlockSpec((d, tn), lambda j: (0, j)),
                  pl.BlockSpec((1, tn), lambda j: (0, j))],
        out_specs=pl.BlockSpec((rows, tn), lambda j: (0, j)),
        out_shape=jax.ShapeDtypeStruct((rows, n), F32),
        compiler_params=_cparams("arbitrary"),
        name="ada_mod",
    )(c, w_ada, b_ada.reshape(1, n))


def _inproj_kernel(x_ref, sc_ref, sh_ref, g_ref, cos_ref, sin_ref, w_ref, wba_ref, o_ref, oba_ref, h_scr,
                   *, n_rope_tiles, n_q_tiles, heads_per_tile, q_scale, norm_rows_chunk):
    j = pl.program_id(1)

    @pl.when(j == 0)
    def _():
        wba = wba_ref[...].astype(BF16)
        per_row_mod = sc_ref.shape[0] != 1

        def norm_rows(ci, carry):
            rs = pl.ds(pl.multiple_of(ci * norm_rows_chunk, norm_rows_chunk), norm_rows_chunk)
            x = x_ref[rs, :]
            ms = jnp.mean(x * x, axis=-1, keepdims=True)
            y = x * lax.rsqrt(ms + NORM_EPS) * g_ref[...]
            sc = sc_ref[rs, :] if per_row_mod else sc_ref[...]
            sh = sh_ref[rs, :] if per_row_mod else sh_ref[...]
            hb = (y * (1.0 + sc) + sh).astype(BF16)
            h_scr[rs, :] = hb
            oba_ref[rs, :] = _dot(hb, wba)
            return carry
        lax.fori_loop(0, x_ref.shape[0] // norm_rows_chunk, norm_rows, 0)

    acc = _dot(h_scr[...], w_ref[...].astype(BF16))

    @pl.when(j < n_rope_tiles)
    def _():
        scale = jnp.where(j < n_q_tiles, q_scale, 1.0).astype(F32)
        cos = cos_ref[...] * scale
        sin = sin_ref[...] * scale
        for h in range(heads_per_tile):
            xh = acc[:, h * HEAD_DIM:(h + 1) * HEAD_DIM]
            o_ref[:, h * HEAD_DIM:(h + 1) * HEAD_DIM] = xh * cos + pltpu.roll(xh, HEAD_DIM // 2, 1) * sin

    @pl.when(j >= n_rope_tiles)
    def _():
        o_ref[...] = acc


def in_proj(x, scale, shift, g, cos, sin, w_in, w_ba, *, tm, d_attn):
    rows, d = x.shape
    tm = min(tm, rows)
    tn = 512
    n_main = (w_in.shape[1] // tn) * tn
    mod_rows = scale.shape[0]
    mod_block = (tm, d) if mod_rows == rows else (1, d)
    mod_map = (lambda i, j: (i, 0)) if mod_rows == rows else (lambda i, j: (0, 0))
    kern = functools.partial(_inproj_kernel, n_rope_tiles=2 * d_attn // tn, n_q_tiles=d_attn // tn,
                             heads_per_tile=tn // HEAD_DIM, q_scale=HEAD_DIM ** -0.5, norm_rows_chunk=min(256, tm))
    return pl.pallas_call(
        kern,
        grid=(rows // tm, n_main // tn),
        in_specs=[pl.BlockSpec((tm, d), lambda i, j: (i, 0)),
                  pl.BlockSpec(mod_block, mod_map),
                  pl.BlockSpec(mod_block, mod_map),
                  pl.BlockSpec((1, d), lambda i, j: (0, 0)),
                  pl.BlockSpec((tm, HEAD_DIM), lambda i, j: (i, 0)),
                  pl.BlockSpec((tm, HEAD_DIM), lambda i, j: (i, 0)),
                  pl.BlockSpec((d, tn), lambda i, j: (0, j)),
                  pl.BlockSpec((d, LANES), lambda i, j: (0, 0))],
        out_specs=[pl.BlockSpec((tm, tn), lambda i, j: (i, j)),
                   pl.BlockSpec((tm, LANES), lambda i, j: (i, 0))],
        out_shape=[jax.ShapeDtypeStruct((rows, n_main), F32),
                   jax.ShapeDtypeStruct((rows, LANES), F32)],
        scratch_shapes=[pltpu.VMEM((tm, d), BF16)],
        compiler_params=_cparams("arbitrary", "arbitrary"),
        name="in_proj",
    )(x, scale, shift, g.reshape(1, d), cos, sin, w_in, w_ba)


ATTN_SPAN = ATTN_BLOCK * max(d for _, d in DILATED_PATTERNS)
ATTN_UNROLL = 16


def _strided_rows(start, size, stride):
    return pl.ds(start, size) if stride == 1 else pl.ds(start, size, stride=stride)


def _attn_prompt_kernel(q_ref, kp_ref, kc_ref, vp_ref, vc_ref, o_ref, kbuf, vbuf, *stat_bufs):
    n = pl.program_id(0)
    span = ATTN_SPAN
    blk = ATTN_BLOCK
    kbuf[0:span, :] = kp_ref[...]
    kbuf[span:2 * span, :] = kc_ref[...]
    vbuf[0:span, :] = vp_ref[...]
    vbuf[span:2 * span, :] = vc_ref[...]
    qi = lax.broadcasted_iota(jnp.int32, (blk, 2 * blk), 0)
    kj = lax.broadcasted_iota(jnp.int32, (blk, 2 * blk), 1)
    dist = qi + blk - kj
    bias = jnp.where(dist < 0, NEG_BIG, jnp.where(dist > blk, NEG_BIG, 0.0)).astype(F32)
    bias_first = jnp.where(kj < blk, NEG_BIG, bias)

    for pi, (_, dil) in enumerate(DILATED_PATTERNS):
        o_buf, l_buf = stat_bufs[2 * pi], stat_bufs[2 * pi + 1]
        unit_span = blk * dil

        def unit(u, carry, dil=dil, unit_span=unit_span, o_buf=o_buf, l_buf=l_buf):
            sub = u // dil
            start = sub * unit_span + (u - sub * dil)
            q = q_ref[_strided_rows(start, blk, dil), :].astype(BF16)
            kcat = kbuf[_strided_rows(span + start - unit_span, 2 * blk, dil), :].astype(BF16)
            vcat = vbuf[_strided_rows(span + start - unit_span, 2 * blk, dil), :].astype(BF16)
            first = jnp.logical_and(n == 0, sub == 0)
            s = _dot_nt(q, kcat) + jnp.where(first, bias_first, bias)
            m = jnp.max(s, axis=-1, keepdims=True)
            p = jnp.exp(s - m)
            l = jnp.sum(p, axis=-1, keepdims=True)
            o_buf[_strided_rows(start, blk, dil), :] = _dot(p.astype(BF16), vcat) / l
            l_buf[_strided_rows(start, blk, dil), :] = jnp.broadcast_to(m + jnp.log(l), (blk, HEAD_DIM))
            return carry
        lax.fori_loop(0, span // blk, unit, 0, unroll=ATTN_UNROLL)

    def merge(bi, carry):
        rs = pl.ds(pl.multiple_of(bi * blk, blk), blk)
        lses = [stat_bufs[2 * pi + 1][rs, :] for pi in range(len(DILATED_PATTERNS))]
        mx = lses[0]
        for x in lses[1:]:
            mx = jnp.maximum(mx, x)
        num = jnp.zeros((blk, HEAD_DIM), F32)
        den = jnp.zeros((blk, HEAD_DIM), F32)
        for pi, x in enumerate(lses):
            wgt = jnp.exp(x - mx)
            num = num + wgt * stat_bufs[2 * pi][rs, :]
            den = den + wgt
        o_ref[rs, :] = num / den
        return carry
    lax.fori_loop(0, span // blk, merge, 0)


def attn_prompt(p, *, n_heads):
    s = p.shape[0]
    span = ATTN_SPAN
    assert s % span == 0
    for window, dil in DILATED_PATTERNS:
        assert window // dil == ATTN_BLOCK and dil & (dil - 1) == 0
    blk = (span, HEAD_DIM)
    n_stats = 2 * len(DILATED_PATTERNS)
    return pl.pallas_call(
        _attn_prompt_kernel,
        grid=(s // span, n_heads),
        in_specs=[pl.BlockSpec(blk, lambda n, h: (n, h)),
                  pl.BlockSpec(blk, lambda n, h: (jnp.maximum(n - 1, 0), n_heads + h)),
                  pl.BlockSpec(blk, lambda n, h: (n, n_heads + h)),
                  pl.BlockSpec(blk, lambda n, h: (jnp.maximum(n - 1, 0), 2 * n_heads + h)),
                  pl.BlockSpec(blk, lambda n, h: (n, 2 * n_heads + h))],
        out_specs=pl.BlockSpec(blk, lambda n, h: (n, h)),
        out_shape=jax.ShapeDtypeStruct((s, n_heads * HEAD_DIM), F32),
        scratch_shapes=[pltpu.VMEM((2 * span, HEAD_DIM), F32), pltpu.VMEM((2 * span, HEAD_DIM), F32)]
                       + [pltpu.VMEM((span, HEAD_DIM), F32)] * n_stats,
        compiler_params=_cparams("arbitrary", "arbitrary"),
        name="attn_prompt",
    )(p, p, p, p, p)


def _attn_sample_kernel(q_ref, kn_ref, vn_ref, kf_ref, vf_ref, kc_ref, vc_ref, cntf_ref, cntc_ref, o_ref,
                        *, n_heads, t_new, half):
    mf = kf_ref.shape[1]
    near = kc_ref.shape[1] // n_heads
    cnt_f = cntf_ref[...]
    cnt_c = cntc_ref[...]
    bias_f = jnp.where(cnt_f > 0.0, 0.0, NEG_BIG)
    bias_c = jnp.where(cnt_c > 0.0, 0.0, NEG_BIG)
    ti = lax.broadcasted_iota(jnp.int32, (t_new, 1), 0)
    for h in range(n_heads):
        sl = slice(h * HEAD_DIM, (h + 1) * HEAD_DIM)
        q = q_ref[:, sl]
        kn = kn_ref[:, sl]
        vn = vn_ref[:, sl]
        q16 = q.astype(BF16)
        far_rows = pl.ds(h, half, stride=n_heads)
        k_far = kf_ref[0, :, 0, far_rows, :].reshape(mf * half, HEAD_DIM).astype(BF16)
        v_far = vf_ref[0, :, 0, far_rows, :].reshape(mf * half, HEAD_DIM).astype(BF16)
        near_rows = pl.ds(h, near, stride=n_heads)
        k_near = kc_ref[0, near_rows, :].astype(BF16)
        v_near = vc_ref[0, near_rows, :].astype(BF16)
        s_f = _dot_nt(q16, k_far) + bias_f
        s_c = _dot_nt(q16, k_near) + bias_c
        m = jnp.maximum(jnp.max(s_f, axis=-1, keepdims=True), jnp.max(s_c, axis=-1, keepdims=True))
        s_n, c_n = [], []
        for t2 in range(t_new):
            diff = ti - t2
            c = jnp.zeros((t_new, 1), F32)
            for window, dil in DILATED_PATTERNS:
                ok = (diff >= 0) & ((diff & (dil - 1)) == 0) & (diff <= window)
                c = c + jnp.where(ok, 1.0, 0.0)
            sn = jnp.where(c > 0.0, jnp.sum(q * kn[t2:t2 + 1, :], axis=-1, keepdims=True), NEG_BIG)
            m = jnp.maximum(m, sn)
            s_n.append(sn)
            c_n.append(c)
        e_f = cnt_f * jnp.exp(s_f - m)
        e_c = cnt_c * jnp.exp(s_c - m)
        l = jnp.sum(e_f, axis=-1, keepdims=True) + jnp.sum(e_c, axis=-1, keepdims=True)
        acc = _dot(e_f.astype(BF16), v_far) + _dot(e_c.astype(BF16), v_near)
        for t2 in range(t_new):
            e_n = c_n[t2] * jnp.exp(s_n[t2] - m)
            l = l + e_n
            acc = acc + e_n * vn[t2:t2 + 1, :]
        o_ref[:, sl] = acc / l


def _sample_key_counts(t_new, cache_len):
    t = np.arange(t_new)[:, None]
    p = np.arange(cache_len)[None, :]
    diff = cache_len + t - p
    cnt = np.zeros((t_new, cache_len), np.float32)
    for window, dil in DILATED_PATTERNS:
        cnt += ((diff % dil == 0) & (diff <= window)).astype(np.float32)
    return cnt


def attn_sample(p, cache_k, cache_v, *, t_new):
    bt = p.shape[0]
    b = bt // t_new
    _, cache_len, n_heads, hd = cache_k.shape
    d_attn = n_heads * hd
    d_max = max(d for _, d in DILATED_PATTERNS)
    half = d_max // 2
    near = max(w for w, d in DILATED_PATTERNS if d != d_max)
    assert hd == HEAD_DIM and cache_len >= max(w for w, _ in DILATED_PATTERNS)
    assert t_new <= half and cache_len % d_max == 0 and near % d_max == 0 and cache_len % near == 0
    for _, dil in DILATED_PATTERNS:
        assert dil & (dil - 1) == 0
    mf = (cache_len - near) // d_max
    cnt = _sample_key_counts(t_new, cache_len)
    far_pos = (np.arange(mf)[:, None] * d_max + np.arange(half)[None, :]).reshape(-1)
    covered = np.zeros(cache_len, bool)
    covered[far_pos] = True
    covered[cache_len - near:] = True
    assert not cnt[:, ~covered].any()
    cnt_far = jnp.asarray(cnt[:, far_pos])
    cnt_near = jnp.asarray(cnt[:, cache_len - near:])
    far_shape = (b, cache_len // d_max, 2, half * n_heads, hd)
    near_shape = (b, cache_len * n_heads, hd)
    kern = functools.partial(_attn_sample_kernel, n_heads=n_heads, t_new=t_new, half=half)
    blk = (t_new, d_attn)
    far_blk = pl.BlockSpec((1, mf, 1, half * n_heads, hd), lambda i: (i, 0, 0, 0, 0))
    near_blk = pl.BlockSpec((1, near * n_heads, hd), lambda i: (i, cache_len // near - 1, 0))
    return pl.pallas_call(
        kern,
        grid=(b,),
        in_specs=[pl.BlockSpec(blk, lambda i: (i, 0)),
                  pl.BlockSpec(blk, lambda i: (i, 1)),
                  pl.BlockSpec(blk, lambda i: (i, 2)),
                  far_blk, far_blk, near_blk, near_blk,
                  pl.BlockSpec(cnt_far.shape, lambda i: (0, 0)),
                  pl.BlockSpec(cnt_near.shape, lambda i: (0, 0))],
        out_specs=pl.BlockSpec(blk, lambda i: (i, 0)),
        out_shape=jax.ShapeDtypeStruct((bt, d_attn), F32),
        compiler_params=_cparams("arbitrary"),
        name="attn_sample",
    )(p, p, p, cache_k.reshape(far_shape), cache_v.reshape(far_shape),
      cache_k.reshape(near_shape), cache_v.reshape(near_shape), cnt_far, cnt_near)


def _causal_conv_silu(x, prev8, cw):
    t = x.shape[0]
    row8 = lax.broadcasted_iota(jnp.int32, (SUBLANES, 1), 0)
    conv = x * cw[CONV_W - 1:CONV_W, :]
    for i in range(1, CONV_W):
        xr = pltpu.roll(x, i, 0)
        pr = pltpu.roll(prev8, i, 0)
        head = jnp.where(row8 < i, pr, xr[:SUBLANES, :])
        xs = head if t == SUBLANES else jnp.concatenate([head, xr[SUBLANES:, :]], axis=0)
        conv = conv + xs * cw[CONV_W - 1 - i:CONV_W - i, :]
    return conv * _sigmoid(conv)


def _l2norm(x):
    return x * lax.rsqrt(jnp.sum(x * x, axis=-1, keepdims=True) + L2_EPS)


def _gated_rmsnorm(o, g_row, z):
    y = o * lax.rsqrt(jnp.mean(o * o, axis=-1, keepdims=True) + NORM_EPS) * g_row
    return y * (z * _sigmoid(z))


def _beta_and_g(ba, alog_row, dtb_row):
    beta = _sigmoid(ba)
    g = -jnp.exp(alog_row) * _softplus(ba + dtb_row)
    return beta, g


def _gdn_prompt_kernel(qkv_ref, z_ref, ba_ref, cw_ref, alog_ref, dtb_ref, gn_ref,
                       o_ref, tail_ref, ssm_ref, s_scr, prev_scr, *, n_heads):
    n = pl.program_id(0)
    c = GDN_CHUNK
    hd = HEAD_DIM

    @pl.when(n == 0)
    def _():
        s_scr[...] = jnp.zeros_like(s_scr)
        prev_scr[...] = jnp.zeros_like(prev_scr)

    x = qkv_ref[...]
    conv = _causal_conv_silu(x, prev_scr[...], cw_ref[...])
    prev_scr[...] = x[c - SUBLANES:, :]
    tail_ref[...] = x[c - SUBLANES:, :]

    beta_all, g_all = _beta_and_g(ba_ref[...], alog_ref[...], dtb_ref[...])
    ri = lax.broadcasted_iota(jnp.int32, (c, c), 0)
    ci = lax.broadcasted_iota(jnp.int32, (c, c), 1)
    tri_incl = ri >= ci
    tri_strict = ri > ci
    eye = jnp.where(ri == ci, 1.0, 0.0).astype(F32)
    blk_xor = ri ^ ci
    tri_b = jnp.where(tri_incl, 1.0, 0.0).astype(BF16)
    g_hi, g_mid, g_lo = _split3(g_all)
    gcum = _dot(tri_b, g_hi) + _dot(tri_b, g_mid) + _dot(tri_b, g_lo)
    gcum_t = gcum.T
    gn = gn_ref[...]

    hs = range(n_heads)
    q = [_l2norm(conv[:, h * hd:(h + 1) * hd]) * (hd ** -0.5) for h in hs]
    k = [_l2norm(conv[:, (n_heads + h) * hd:(n_heads + h + 1) * hd]) for h in hs]
    v = [conv[:, (2 * n_heads + h) * hd:(2 * n_heads + h + 1) * hd] for h in hs]
    beta = [beta_all[:, h:h + 1] for h in hs]
    gc = [gcum[:, n_heads + h:n_heads + h + 1] for h in hs]
    gr = [gcum_t[n_heads + h:n_heads + h + 1, :] for h in hs]
    g_last = [gcum[c - 1:c, n_heads + h:n_heads + h + 1] for h in hs]
    decay = [jnp.where(tri_incl, jnp.exp(gc[h] - gr[h]), 0.0) for h in hs]
    kb = [k[h] * beta[h] for h in hs]
    k16 = [k[h].astype(BF16) for h in hs]
    a = [jnp.where(tri_strict, _dot_nt(kb[h].astype(BF16), k16[h]) * decay[h], 0.0) for h in hs]
    qk = [_dot_nt(q[h].astype(BF16), k16[h]) * decay[h] for h in hs]
    t_inv = [eye - jnp.where((blk_xor >> 1) == 0, a[h], 0.0) for h in hs]
    for lvl in range(2, int(math.log2(c)) + 1):
        join = (blk_xor >> (lvl - 1)) == 1
        t16 = [t_inv[h].astype(BF16) for h in hs]
        lt = [_dot(jnp.where(join, a[h], 0.0).astype(BF16), t16[h]).astype(BF16) for h in hs]
        t_inv = [t_inv[h] - _dot(t16[h], lt[h]) for h in hs]
    rhs = [jnp.concatenate([v[h] * beta[h], kb[h] * jnp.exp(gc[h])], axis=-1).astype(BF16) for h in hs]
    sol = [_dot(t_inv[h].astype(BF16), rhs[h]) for h in hs]
    s_old = [s_scr[h] for h in hs]
    s16 = [s_old[h].astype(BF16) for h in hs]
    v_new = [sol[h][:, :hd] - _dot(sol[h][:, hd:].astype(BF16), s16[h]) for h in hs]
    v16 = [v_new[h].astype(BF16) for h in hs]
    o = [_dot((q[h] * jnp.exp(gc[h])).astype(BF16), s16[h]) + _dot(qk[h].astype(BF16), v16[h]) for h in hs]
    k_dec = [(k[h] * jnp.exp(g_last[h] - gc[h])).T.astype(BF16) for h in hs]
    for h in hs:
        s_scr[h] = s_old[h] * jnp.exp(g_last[h]) + _dot(k_dec[h], v16[h])
        o_ref[:, h * hd:(h + 1) * hd] = _gated_rmsnorm(o[h], gn, z_ref[:, h * hd:(h + 1) * hd])

    @pl.when(n == pl.num_programs(0) - 1)
    def _():
        ssm_ref[...] = s_scr[...]


def gdn_prompt(p, pba, conv_w, alog_row, dtb_row, g_norm, *, n_heads, qkv_col_block, z_col_block):
    s = p.shape[0]
    c = GDN_CHUNK
    hd = HEAD_DIM
    conv_dim = 3 * n_heads * hd
    kern = functools.partial(_gdn_prompt_kernel, n_heads=n_heads)
    return pl.pallas_call(
        kern,
        grid=(s // c,),
        in_specs=[pl.BlockSpec((c, conv_dim), lambda n: (n, qkv_col_block)),
                  pl.BlockSpec((c, n_heads * hd), lambda n: (n, z_col_block)),
                  pl.BlockSpec((c, LANES), lambda n: (n, 0)),
                  pl.BlockSpec((CONV_W, conv_dim), lambda n: (0, 0)),
                  pl.BlockSpec((1, LANES), lambda n: (0, 0)),
                  pl.BlockSpec((1, LANES), lambda n: (0, 0)),
                  pl.BlockSpec((1, hd), lambda n: (0, 0))],
        out_specs=[pl.BlockSpec((c, n_heads * hd), lambda n: (n, 0)),
                   pl.BlockSpec((SUBLANES, conv_dim), lambda n: (0, 0)),
                   pl.BlockSpec((n_heads, hd, hd), lambda n: (0, 0, 0))],
        out_shape=[jax.ShapeDtypeStruct((s, n_heads * hd), F32),
                   jax.ShapeDtypeStruct((SUBLANES, conv_dim), F32),
                   jax.ShapeDtypeStruct((n_heads, hd, hd), F32)],
        scratch_shapes=[pltpu.VMEM((n_heads, hd, hd), F32), pltpu.VMEM((SUBLANES, conv_dim), F32)],
        compiler_params=_cparams("arbitrary"),
        name="gdn_prompt",
    )(p, p, pba, conv_w, alog_row, dtb_row, g_norm.reshape(1, hd))


def _gdn_sample_kernel(qkv_ref, z_ref, ba_ref, prev_ref, s0_ref, cw_ref, alog_ref, dtb_ref, gn_ref,
                       o_ref, ssm_ref, *, n_heads, t_new, n_seq):
    hd = HEAD_DIM
    cw = cw_ref[...]
    gn = gn_ref[...]
    beta_all, g_all = _beta_and_g(ba_ref[...], alog_ref[...], dtb_ref[...])
    eg_all = jnp.exp(g_all)
    pad = jnp.zeros((hd - 2 * t_new, hd), F32)
    chains = [(bi, h) for bi in range(n_seq) for h in range(n_heads)]
    conv = [_causal_conv_silu(qkv_ref[bi * t_new:(bi + 1) * t_new, :], prev_ref[bi], cw) for bi in range(n_seq)]
    q = {(bi, h): _l2norm(conv[bi][:, h * hd:(h + 1) * hd]) * (hd ** -0.5) for bi, h in chains}
    k = {(bi, h): _l2norm(conv[bi][:, (n_heads + h) * hd:(n_heads + h + 1) * hd]) for bi, h in chains}
    v = {(bi, h): conv[bi][:, (2 * n_heads + h) * hd:(2 * n_heads + h + 1) * hd] for bi, h in chains}
    kq_t = {c: jnp.concatenate([k[c], q[c], pad], axis=0).T for c in chains}
    s = {(bi, h): s0_ref[bi, h] for bi, h in chains}
    rows = {c: [] for c in chains}
    for t in range(t_new):
        for bi, h in chains:
            c = (bi, h)
            r = bi * t_new + t
            kb = jnp.broadcast_to(kq_t[c][:, t:t + 1], (hd, hd))
            qb = jnp.broadcast_to(kq_t[c][:, t_new + t:t_new + t + 1], (hd, hd))
            eg = eg_all[r:r + 1, n_heads + h:n_heads + h + 1]
            bt = beta_all[r:r + 1, h:h + 1]
            sd = s[c] * eg
            ks = jnp.sum(sd * kb, axis=0, keepdims=True)
            v_new = bt * (v[c][t:t + 1, :] - ks)
            s[c] = sd + kb * v_new
            rows[c].append(jnp.sum(s[c] * qb, axis=0, keepdims=True))
    for bi, h in chains:
        ssm_ref[bi, h] = s[(bi, h)]
        o = jnp.concatenate(rows[(bi, h)], axis=0)
        rs = slice(bi * t_new, (bi + 1) * t_new)
        o_ref[rs, h * hd:(h + 1) * hd] = _gated_rmsnorm(o, gn, z_ref[rs, h * hd:(h + 1) * hd])


def gdn_sample(p, pba, prev8, s0, conv_w, alog_row, dtb_row, g_norm, *, n_heads, t_new, qkv_col_block, z_col_block):
    bt = p.shape[0]
    b = bt // t_new
    hd = HEAD_DIM
    conv_dim = 3 * n_heads * hd
    n_seq = math.gcd(GDN_SAMPLE_SEQS, b)
    rows = n_seq * t_new
    kern = functools.partial(_gdn_sample_kernel, n_heads=n_heads, t_new=t_new, n_seq=n_seq)
    return pl.pallas_call(
        kern,
        grid=(b // n_seq,),
        in_specs=[pl.BlockSpec((rows, conv_dim), lambda i: (i, qkv_col_block)),
                  pl.BlockSpec((rows, n_heads * hd), lambda i: (i, z_col_block)),
                  pl.BlockSpec((rows, LANES), lambda i: (i, 0)),
                  pl.BlockSpec((n_seq, SUBLANES, conv_dim), lambda i: (i, 0, 0)),
                  pl.BlockSpec((n_seq, n_heads, hd, hd), lambda i: (i, 0, 0, 0)),
                  pl.BlockSpec((CONV_W, conv_dim), lambda i: (0, 0)),
                  pl.BlockSpec((1, LANES), lambda i: (0, 0)),
                  pl.BlockSpec((1, LANES), lambda i: (0, 0)),
                  pl.BlockSpec((1, hd), lambda i: (0, 0))],
        out_specs=[pl.BlockSpec((rows, n_heads * hd), lambda i: (i, 0)),
                   pl.BlockSpec((n_seq, n_heads, hd, hd), lambda i: (i, 0, 0, 0))],
        out_shape=[jax.ShapeDtypeStruct((bt, n_heads * hd), F32),
                   jax.ShapeDtypeStruct((b, n_heads, hd, hd), F32)],
        compiler_params=_cparams("arbitrary"),
        name="gdn_sample",
    )(p, p, pba, prev8, s0, conv_w, alog_row, dtb_row, g_norm.reshape(1, hd))


def _outproj_kernel(oa_ref, ob_ref, x_ref, gate_ref, sc_ref, sh_ref, gpost_ref, gpre_ref, wo_ref, wr_hi_ref, wr_lo_ref,
                    br_ref, x1_ref, h2_ref, tw_ref, ti_ref):
    d_attn = oa_ref.shape[1]
    mix = (_dot(oa_ref[...].astype(BF16), wo_ref[:d_attn, :]) + _dot(ob_ref[...].astype(BF16), wo_ref[d_attn:, :]))
    y = mix * lax.rsqrt(jnp.mean(mix * mix, axis=-1, keepdims=True) + NORM_EPS) * gpost_ref[...]
    x1 = x_ref[...] + gate_ref[...] * y
    x1_ref[...] = x1
    hn = x1 * lax.rsqrt(jnp.mean(x1 * x1, axis=-1, keepdims=True) + NORM_EPS) * gpre_ref[...]
    h2 = hn * (1.0 + sc_ref[...]) + sh_ref[...]
    h2_ref[...] = _pack_bf16_pairs(h2)

    h_hi = h2.astype(BF16)
    h_lo = (h2 - h_hi.astype(F32)).astype(BF16)
    logits = (_dot(h_hi, wr_hi_ref[...]) + _dot(h_hi, wr_lo_ref[...]) + _dot(h_lo, wr_hi_ref[...])) + br_ref[...]
    lane = lax.broadcasted_iota(jnp.int32, logits.shape, 1)
    tops, idxs = [], []
    for _ in range(TOP_K):
        mx = jnp.max(logits, axis=-1, keepdims=True)
        ix = jnp.min(jnp.where(logits == mx, lane, LANES), axis=-1, keepdims=True)
        tops.append(mx)
        idxs.append(ix)
        logits = jnp.where(lane == ix, 2.0 * NEG_BIG, logits)
    es = [jnp.exp(t - tops[0]) for t in tops]
    tot = es[0]
    for e in es[1:]:
        tot = tot + e
    tw = jnp.zeros(logits.shape, F32)
    ti = jnp.zeros(logits.shape, jnp.int32)
    for kk in range(TOP_K):
        tw = jnp.where(lane == kk, es[kk] / tot, tw)
        ti = jnp.where(lane == kk, idxs[kk], ti)
    tw_ref[...] = tw
    ti_ref[...] = ti


def out_proj(o_a, o_b, x, gate, scale, shift, g_post, g_pre, wo16, wr_hi, wr_lo, br_row, *, tm):
    rows, d = x.shape
    tm = min(tm, rows)
    d_attn = o_a.shape[1]
    mod_rows = gate.shape[0]
    mod_block = (tm, d) if mod_rows == rows else (1, d)
    mod_map = (lambda i: (i, 0)) if mod_rows == rows else (lambda i: (0, 0))
    half = pl.BlockSpec((tm, d_attn), lambda i: (i, 0))
    full = pl.BlockSpec((tm, d), lambda i: (i, 0))
    row_d = pl.BlockSpec((1, d), lambda i: (0, 0))
    lane_blk = pl.BlockSpec((tm, LANES), lambda i: (i, 0))
    in_specs = [half, pl.BlockSpec((tm, d - d_attn), lambda i: (i, 0)), full,
                pl.BlockSpec(mod_block, mod_map), pl.BlockSpec(mod_block, mod_map), pl.BlockSpec(mod_block, mod_map),
                row_d, row_d,
                pl.BlockSpec((d, d), lambda i: (0, 0)),
                pl.BlockSpec((d, LANES), lambda i: (0, 0)), pl.BlockSpec((d, LANES), lambda i: (0, 0)),
                pl.BlockSpec((1, LANES), lambda i: (0, 0))]
    args = [o_a, o_b, x, gate, scale, shift, g_post.reshape(1, d), g_pre.reshape(1, d), wo16, wr_hi, wr_lo, br_row]
    return pl.pallas_call(
        _outproj_kernel,
        grid=(rows // tm,),
        in_specs=in_specs,
        out_specs=[full, pl.BlockSpec((tm, d // 2), lambda i: (i, 0)), lane_blk, lane_blk],
        out_shape=[jax.ShapeDtypeStruct((rows, d), F32), jax.ShapeDtypeStruct((rows, d // 2), jnp.uint32),
                   jax.ShapeDtypeStruct((rows, LANES), F32), jax.ShapeDtypeStruct((rows, LANES), jnp.int32)],
        compiler_params=_cparams("arbitrary"),
        name="out_proj",
    )(*args)


def _pack_bf16_pairs(x):
    half = x.shape[1] // 2
    bits = pltpu.bitcast(x.astype(BF16).astype(F32), jnp.uint32)
    return bits[:, :half] | (bits[:, half:] >> 16)


def _unpack_bf16_pairs(w):
    hi = pltpu.bitcast(w & jnp.uint32(0xFFFF0000), F32).astype(BF16)
    lo = pltpu.bitcast(w << 16, F32).astype(BF16)
    return hi, lo


DISPATCH_TOKENS = 256
ROW_DMA_UNROLL = 8


def _dispatch_kernel(pos_ref, zs_ref, h_ref, x_hbm, zbuf, sem_z, sem):
    i = pl.program_id(0)
    tt = h_ref.shape[0]
    sb = zbuf.shape[0]
    n_items = zs_ref.shape[0]
    n_sub = MOE_BLOCK_ROWS // sb

    def zero_copy(w, s):
        row0 = pl.multiple_of((w * n_sub + s) * sb, sb)
        return pltpu.make_async_copy(zbuf, x_hbm.at[pl.ds(row0, sb), :], sem_z)

    @pl.when(i == 0)
    def _():
        zbuf[...] = jnp.zeros(zbuf.shape, zbuf.dtype)

        def start(w, carry):
            lax.fori_loop(zs_ref[w], n_sub, lambda s, c: (zero_copy(w, s).start(), c)[1], 0)
            return carry
        lax.fori_loop(0, n_items, start, 0)

        def wait(w, carry):
            lax.fori_loop(zs_ref[w], n_sub, lambda s, c: (zero_copy(w, s).wait(), c)[1], 0)
            return carry
        lax.fori_loop(0, n_items, wait, 0)

    base = i * (tt * TOP_K)

    def row_copy(t, kk):
        dst = pos_ref[base + t * TOP_K + kk]
        return pltpu.make_async_copy(h_ref.at[pl.ds(t, 1), :], x_hbm.at[pl.ds(dst, 1), :], sem)

    for kk in range(TOP_K):
        def start(t, carry, kk=kk):
            row_copy(t, kk).start()
            return carry
        lax.fori_loop(0, tt, start, 0, unroll=ROW_DMA_UNROLL)
    for kk in range(TOP_K):
        def wait(t, carry, kk=kk):
            row_copy(t, kk).wait()
            return carry
        lax.fori_loop(0, tt, wait, 0, unroll=ROW_DMA_UNROLL)


def moe_dispatch(h2p, pos_flat, zero_from):
    n_tok, half = h2p.shape
    tt = math.gcd(DISPATCH_TOKENS, n_tok)
    assert tt % SUBLANES == 0
    n_items = zero_from.shape[0]
    grid_spec = pltpu.PrefetchScalarGridSpec(
        num_scalar_prefetch=2,
        grid=(n_tok // tt,),
        in_specs=[pl.BlockSpec((tt, half), lambda i, pos, zs: (i, 0))],
        out_specs=pl.BlockSpec(memory_space=pl.ANY),
        scratch_shapes=[pltpu.VMEM((MOE_SUB_ROWS, half), jnp.uint32), pltpu.SemaphoreType.DMA(()),
                        pltpu.SemaphoreType.DMA(())],
    )
    return pl.pallas_call(
        _dispatch_kernel,
        grid_spec=grid_spec,
        out_shape=jax.ShapeDtypeStruct((n_items * MOE_BLOCK_ROWS, half), jnp.uint32),
        compiler_params=_cparams("arbitrary"),
        name="moe_dispatch",
    )(pos_flat, zero_from, h2p)


def _moe_kernel(e_ref, ns_ref, x_ref, wg_ref, wu_ref, bg_ref, bu_ref, wd_ref, bd_ref, o_ref, xb_scr):
    w = pl.program_id(0)
    j = pl.program_id(1)
    ns = ns_ref[w]
    sb = MOE_SUB_ROWS
    n_sub = MOE_BLOCK_ROWS // sb
    half = x_ref.shape[1]

    @pl.when(j == 0)
    def _():
        def zero(s, carry):
            r0 = pl.multiple_of(s * sb, sb)
            o_ref[pl.ds(r0, sb), :] = jnp.zeros((sb, o_ref.shape[1]), F32)
            return carry
        lax.fori_loop(ns, n_sub, zero, 0)

        def unpack(s, carry):
            rs = pl.ds(pl.multiple_of(s * sb, sb), sb)
            hi, lo = _unpack_bf16_pairs(x_ref[rs, :])
            xb_scr[rs, :half] = hi
            xb_scr[rs, half:] = lo
            o_ref[rs, :] = jnp.broadcast_to(bd_ref[0], (sb, o_ref.shape[1]))
            return carry
        lax.fori_loop(0, ns, unpack, 0)

    def compute(n_rows):
        xs = xb_scr[0:n_rows, :]
        gate = jnp.minimum(_dot(xs, wg_ref[0].astype(BF16)) + bg_ref[0], SWIGLU_LIMIT)
        up = jnp.clip(_dot(xs, wu_ref[0].astype(BF16)) + bu_ref[0], -SWIGLU_LIMIT, SWIGLU_LIMIT)
        act = (gate * _sigmoid(SWIGLU_ALPHA * gate) * (up + 1.0)).astype(BF16)
        wd = wd_ref[0].astype(BF16)
        dc = MOE_DOWN_CHUNK
        for c0 in range(0, o_ref.shape[1], dc):
            o_ref[0:n_rows, c0:c0 + dc] += _dot(act, wd[:, c0:c0 + dc])

    for n in range(1, n_sub + 1):
        pl.when(ns == n)(functools.partial(compute, n * sb))


def moe_experts(x_sorted, item_e, item_ns, w_gate_up, b_gate_up, w_down, b_down):
    n_rows, half = x_sorted.shape
    d = 2 * half
    n_e, _, two_ff = w_gate_up.shape
    d_ff = two_ff // 2
    bm = MOE_BLOCK_ROWS
    tf = MOE_FF_TILE
    n_f = d_ff // tf
    n_items = n_rows // bm

    def jj(j, ns, w):
        return jnp.where(ns[w] > 0, j, n_f - 1)

    grid_spec = pltpu.PrefetchScalarGridSpec(
        num_scalar_prefetch=2,
        grid=(n_items, n_f),
        in_specs=[pl.BlockSpec((bm, half), lambda w, j, e, ns: (w, 0)),
                  pl.BlockSpec((1, d, tf), lambda w, j, e, ns: (e[w], 0, jj(j, ns, w))),
                  pl.BlockSpec((1, d, tf), lambda w, j, e, ns: (e[w], 0, n_f + jj(j, ns, w))),
                  pl.BlockSpec((1, 1, tf), lambda w, j, e, ns: (e[w], 0, jj(j, ns, w))),
                  pl.BlockSpec((1, 1, tf), lambda w, j, e, ns: (e[w], 0, n_f + jj(j, ns, w))),
                  pl.BlockSpec((1, tf, d), lambda w, j, e, ns: (e[w], jj(j, ns, w), 0)),
                  pl.BlockSpec((1, 1, d), lambda w, j, e, ns: (e[w], 0, 0))],
        out_specs=pl.BlockSpec((bm, d), lambda w, j, e, ns: (w, 0)),
        scratch_shapes=[pltpu.VMEM((bm, d), BF16)],
    )
    return pl.pallas_call(
        _moe_kernel,
        grid_spec=grid_spec,
        out_shape=jax.ShapeDtypeStruct((n_rows, d), F32),
        compiler_params=_cparams("arbitrary", "arbitrary"),
        name="moe_experts",
    )(item_e, item_ns, x_sorted, w_gate_up, w_gate_up, b_gate_up.reshape(n_e, 1, two_ff),
      b_gate_up.reshape(n_e, 1, two_ff), w_down, b_down.reshape(n_e, 1, d))


def moe_routing(top_e):
    n_tok = top_e.shape[0]
    n_asg = n_tok * TOP_K
    bm = MOE_BLOCK_ROWS
    sb = MOE_SUB_ROWS
    n_items = n_asg // bm + N_EXPERTS
    i32 = jnp.int32
    onehot = jnp.sum((top_e[:, :, None] == jnp.arange(N_EXPERTS, dtype=i32)[None, None, :]).astype(i32), axis=1)
    counts = jnp.sum(onehot, axis=0)
    earlier = jnp.cumsum(onehot, axis=0) - onehot
    rank = jnp.take_along_axis(earlier, top_e, axis=1)
    n_it = (counts + bm - 1) // bm
    per = ((counts + jnp.maximum(n_it, 1) - 1) // jnp.maximum(n_it, 1) + sb - 1) // sb * sb
    per = jnp.maximum(per, sb)
    it_end = jnp.cumsum(n_it)
    it_base = it_end - n_it
    per_t = per[top_e]
    k_idx = rank // per_t
    pos = ((it_base[top_e] + k_idx) * bm + rank - k_idx * per_t).astype(i32)
    w = jnp.arange(n_items, dtype=i32)
    total = it_end[-1]
    e_raw = jnp.minimum(jnp.sum((it_end[None, :] <= w[:, None]).astype(i32), axis=1), N_EXPERTS - 1)
    rows = jnp.clip(counts[e_raw] - (w - it_base[e_raw]) * per[e_raw], 0, per[e_raw])
    rows = jnp.where(w < total, rows, 0)
    item_ns = (rows + sb - 1) // sb
    item_e = jnp.where(w < total, e_raw, e_raw[jnp.maximum(total - 1, 0)])
    zero_from = jnp.maximum(item_ns - 1, 0)
    return pos, item_e.astype(i32), item_ns.astype(i32), zero_from.astype(i32)


def _finish_kernel(pos_ref, x1_ref, tw_ref, gate_ref, g_ref, y_hbm, o_ref, ybuf, sem):
    i = pl.program_id(0)
    n_tiles = pl.num_programs(0)
    tm, d = x1_ref.shape

    def row_copy(base, slot, kk, t):
        src = pos_ref[base + t * TOP_K + kk]
        return pltpu.make_async_copy(y_hbm.at[pl.ds(src, 1), :], ybuf.at[slot, kk, pl.ds(t, 1), :], sem.at[slot])

    def issue(tile, slot):
        base = tile * (tm * TOP_K)
        for kk in range(TOP_K):
            def body(t, carry, kk=kk):
                row_copy(base, slot, kk, t).start()
                return carry
            lax.fori_loop(0, tm, body, 0, unroll=ROW_DMA_UNROLL)

    def wait(tile, slot):
        base = tile * (tm * TOP_K)
        for kk in range(TOP_K):
            def body(t, carry, kk=kk):
                row_copy(base, slot, kk, t).wait()
                return carry
            lax.fori_loop(0, tm, body, 0, unroll=ROW_DMA_UNROLL)

    @pl.when(i == 0)
    def _():
        issue(0, 0)

    @pl.when(i + 1 < n_tiles)
    def _():
        issue(i + 1, (i + 1) % 2)

    slot = i % 2
    wait(i, slot)
    tw = tw_ref[...]
    ff = tw[:, 0:1] * ybuf[slot, 0]
    for kk in range(1, TOP_K):
        ff = ff + tw[:, kk:kk + 1] * ybuf[slot, kk]
    y = ff * lax.rsqrt(jnp.mean(ff * ff, axis=-1, keepdims=True) + NORM_EPS) * g_ref[...]
    o_ref[...] = x1_ref[...] + gate_ref[...] * y


def finish(x1, y_sorted, pos, tw, gate, g_post, *, tm):
    rows, d = x1.shape
    tm = min(tm, rows)
    mod_rows = gate.shape[0]
    mod_block = (tm, d) if mod_rows == rows else (1, d)
    mod_map = (lambda i, pos: (i, 0)) if mod_rows == rows else (lambda i, pos: (0, 0))
    grid_spec = pltpu.PrefetchScalarGridSpec(
        num_scalar_prefetch=1,
        grid=(rows // tm,),
        in_specs=[pl.BlockSpec((tm, d), lambda i, pos: (i, 0)),
                  pl.BlockSpec((tm, LANES), lambda i, pos: (i, 0)),
                  pl.BlockSpec(mod_block, mod_map),
                  pl.BlockSpec((1, d), lambda i, pos: (0, 0)),
                  pl.BlockSpec(memory_space=pl.ANY)],
        out_specs=pl.BlockSpec((tm, d), lambda i, pos: (i, 0)),
        scratch_shapes=[pltpu.VMEM((2, TOP_K, tm, d), F32), pltpu.SemaphoreType.DMA((2,))],
    )
    return pl.pallas_call(
        _finish_kernel,
        grid_spec=grid_spec,
        out_shape=jax.ShapeDtypeStruct((rows, d), F32),
        compiler_params=_cparams("arbitrary"),
        name="finish",
    )(pos, x1, tw, gate, g_post.reshape(1, d), y_sorted)


def _rope_tables(pos):
    half = HEAD_DIM // 2
    inv_freq = ROPE_THETA ** (-np.arange(half, dtype=np.float64) / half)
    ang = np.asarray(pos, np.float64)[:, None] * inv_freq[None, :]
    cos = np.cos(ang)
    sin = np.sin(ang)
    return (jnp.asarray(np.concatenate([cos, cos], axis=-1), F32),
            jnp.asarray(np.concatenate([-sin, sin], axis=-1), F32))


def _lane_row(vals, offset):
    return jnp.zeros((1, LANES), F32).at[0, offset:offset + vals.shape[0]].set(vals.astype(F32))


def kernel(x_prompt, x_sample, c_prompt, c_sample, cache_win_k, cache_win_v, state_conv, state_ssm, w_ada, b_ada, g_pre_mix, g_post_mix, g_pre_ffn, g_post_ffn, w_in, conv_w, a_log, dt_bias, g_gdn_norm, w_out, w_router, b_router, w_gate_up, b_gate_up, w_down, b_down):
    depth = w_ada.shape[0]
    assert depth == 1 and x_prompt.shape[0] == 1
    _, s_len, d = x_prompt.shape
    b_s, t_s, _ = x_sample.shape
    n_heads_a = cache_win_k.shape[3]
    d_attn = n_heads_a * HEAD_DIM
    n_heads_b = state_ssm.shape[2]
    conv_dim = 3 * n_heads_b * HEAD_DIM
    n_samp = b_s * t_s
    win_p = min(max(w for w, _ in DILATED_PATTERNS), s_len)

    c_all = jnp.concatenate([c_prompt, c_sample], axis=0)
    c_rows = -(-c_all.shape[0] // SUBLANES) * SUBLANES
    c_all = jnp.pad(c_all, ((0, c_rows - c_all.shape[0]), (0, 0)))
    mod = ada_mod(c_all, w_ada[0], b_ada[0])
    mod_p = [mod[0:1, i * d:(i + 1) * d] for i in range(6)]
    mod_s = [jnp.repeat(mod[1:1 + b_s, i * d:(i + 1) * d], t_s, axis=0) for i in range(6)]

    n_main = d_attn * 3 + conv_dim + n_heads_b * HEAD_DIM
    w_ba = jnp.pad(w_in[0][:, n_main:], ((0, 0), (0, LANES - (w_in.shape[2] - n_main))))
    cos_p, sin_p = _rope_tables(np.arange(s_len))
    cos_s, sin_s = _rope_tables(PAST_LEN + (np.arange(n_samp) % t_s))
    xp = x_prompt[0]
    xs = x_sample.reshape(n_samp, d)
    p_p, pba_p = in_proj(xp, mod_p[1], mod_p[0], g_pre_mix[0], cos_p, sin_p, w_in[0], w_ba, tm=1024, d_attn=d_attn)
    p_s, pba_s = in_proj(xs, mod_s[1], mod_s[0], g_pre_mix[0], cos_s, sin_s, w_in[0], w_ba, tm=512, d_attn=d_attn)

    oa_p = attn_prompt(p_p, n_heads=n_heads_a)
    oa_s = attn_sample(p_s, cache_win_k[0], cache_win_v[0], t_new=t_s)

    alog_row = _lane_row(a_log[0], n_heads_b)
    dtb_row = _lane_row(dt_bias[0], n_heads_b)
    qkv_blk = (3 * d_attn) // conv_dim
    z_blk = (3 * d_attn + conv_dim) // (n_heads_b * HEAD_DIM)
    assert qkv_blk * conv_dim == 3 * d_attn and z_blk * n_heads_b * HEAD_DIM == 3 * d_attn + conv_dim
    ob_p, tail_p, ssm_p = gdn_prompt(p_p, pba_p, conv_w[0], alog_row, dtb_row, g_gdn_norm[0],
                                     n_heads=n_heads_b, qkv_col_block=qkv_blk, z_col_block=z_blk)
    prev8 = jnp.pad(state_conv[0], ((0, 0), (SUBLANES - (CONV_W - 1), 0), (0, 0)))
    ob_s, ssm_s = gdn_sample(p_s, pba_s, prev8, state_ssm[0], conv_w[0], alog_row, dtb_row, g_gdn_norm[0],
                             n_heads=n_heads_b, t_new=t_s, qkv_col_block=qkv_blk, z_col_block=z_blk)

    wo16 = w_out[0].astype(BF16)
    wr = jnp.pad(w_router[0], ((0, 0), (0, LANES - N_EXPERTS)))
    wr_hi = wr.astype(BF16)
    wr_lo = (wr - wr_hi.astype(F32)).astype(BF16)
    br_row = jnp.full((1, LANES), NEG_BIG, F32).at[0, :N_EXPERTS].set(b_router[0])
    x1_p, h2_p, tw_p, ti_p = out_proj(oa_p, ob_p, xp, mod_p[2], mod_p[4], mod_p[3], g_post_mix[0], g_pre_ffn[0],
                                      wo16, wr_hi, wr_lo, br_row, tm=256)
    x1_s, h2_s, tw_s, ti_s = out_proj(oa_s, ob_s, xs, mod_s[2], mod_s[4], mod_s[3], g_post_mix[0], g_pre_ffn[0],
                                      wo16, wr_hi, wr_lo, br_row, tm=256)

    h2 = jnp.concatenate([h2_p, h2_s], axis=0)
    top_e = jnp.concatenate([ti_p[:, :TOP_K], ti_s[:, :TOP_K]], axis=0)
    pos, item_e, item_ns, zero_from = moe_routing(top_e)
    x_sorted = moe_dispatch(h2, pos.reshape(-1), zero_from)
    y_sorted = moe_experts(x_sorted, item_e, item_ns, w_gate_up[0], b_gate_up[0], w_down[0], b_down[0])
    y_p = finish(x1_p, y_sorted, pos[:s_len].reshape(-1), tw_p, mod_p[5], g_post_ffn[0], tm=128)
    y_s = finish(x1_s, y_sorted, pos[s_len:].reshape(-1), tw_s, mod_s[5], g_post_ffn[0], tm=128)

    hd = HEAD_DIM
    k_p = p_p[s_len - win_p:, d_attn:2 * d_attn].reshape(1, 1, win_p, n_heads_a, hd)
    v_p = p_p[s_len - win_p:, 2 * d_attn:3 * d_attn].reshape(1, 1, win_p, n_heads_a, hd)
    conv_p = tail_p[SUBLANES - (CONV_W - 1):].reshape(1, 1, CONV_W - 1, conv_dim)
    k_s = p_s[:, d_attn:2 * d_attn].reshape(1, b_s, t_s, n_heads_a, hd)
    v_s = p_s[:, 2 * d_attn:3 * d_attn].reshape(1, b_s, t_s, n_heads_a, hd)
    conv_s = p_s.reshape(b_s, t_s, -1)[:, t_s - (CONV_W - 1):, 3 * d_attn:3 * d_attn + conv_dim][None]
    return (y_p[None], y_s.reshape(b_s, t_s, d), k_p, v_p, conv_p, ssm_p[None, None],
            k_s, v_s, conv_s, ssm_s[None])
```

```python
import functools
import math

import numpy as np
import jax
import jax.numpy as jnp
from jax import lax
from jax.experimental import pallas as pl
from jax.experimental.pallas import tpu as pltpu

F32 = jnp.float32
BF16 = jnp.bfloat16

HEAD_DIM = 128
DILATED_PATTERNS = ((128, 1), (512, 4), (2048, 16))
ATTN_BLOCK = 128
PAST_LEN = 2048
ROPE_THETA = 10000.0
CONV_W = 4
N_EXPERTS = 32
TOP_K = 4
SWIGLU_LIMIT = 7.0
SWIGLU_ALPHA = 1.702
NORM_EPS = 1e-6
L2_EPS = 1e-6

LANES = 128
SUBLANES = 8
VMEM_LIMIT_BYTES = 56 * 1024 * 1024

NEG_BIG = -1e30

GDN_CHUNK = 128
GDN_SAMPLE_SEQS = 2
MOE_BLOCK_ROWS = 1024
MOE_SUB_ROWS = 128
MOE_FF_TILE = 256
MOE_DOWN_CHUNK = 512


def _cparams(*sem):
    return pltpu.CompilerParams(dimension_semantics=sem, vmem_limit_bytes=VMEM_LIMIT_BYTES)


def _sigmoid(x):
    return 1.0 / (1.0 + jnp.exp(-x))


def _softplus(x):
    return jnp.maximum(x, 0.0) + jnp.log(1.0 + jnp.exp(-jnp.abs(x)))


def _dot(a, b):
    return jnp.dot(a, b, preferred_element_type=F32)


def _dot_nt(a, b):
    return lax.dot_general(a, b, (((1,), (1,)), ((), ())), preferred_element_type=F32)


def _split3(x):
    hi = x.astype(BF16)
    r = x - hi.astype(F32)
    mid = r.astype(BF16)
    lo = (r - mid.astype(F32)).astype(BF16)
    return hi, mid, lo


def _ada_kernel(c_ref, w_ref, b_ref, o_ref):
    c = c_ref[...]
    a = (c * _sigmoid(c)).astype(BF16)
    o_ref[...] = _dot(a, w_ref[...].astype(BF16)) + b_ref[...]


def ada_mod(c, w_ada, b_ada):
    rows, d = c.shape
    n = w_ada.shape[1]
    tn = 1024
    return pl.pallas_call(
        _ada_kernel,
        grid=(n // tn,),
        in_specs=[pl.BlockSpec((rows, d), lambda j: (0, 0)),
                  pl.BlockSpec((d, tn), lambda j: (0, j)),
                  pl.BlockSpec((1, tn), lambda j: (0, j))],
        out_specs=pl.BlockSpec((rows, tn), lambda j: (0, j)),
        out_shape=jax.ShapeDtypeStruct((rows, n), F32),
        compiler_params=_cparams("arbitrary"),
        name="ada_mod",
    )(c, w_ada, b_ada.reshape(1, n))


def _inproj_kernel(x_ref, sc_ref, sh_ref, g_ref, cos_ref, sin_ref, w_ref, wba_ref, o_ref, oba_ref, h_scr,
                   *, n_rope_tiles, n_q_tiles, heads_per_tile, q_scale, norm_rows_chunk):
    j = pl.program_id(1)

    @pl.when(j == 0)
    def _():
        wba = wba_ref[...].astype(BF16)
        per_row_mod = sc_ref.shape[0] != 1

        def norm_rows(ci, carry):
            rs = pl.ds(pl.multiple_of(ci * norm_rows_chunk, norm_rows_chunk), norm_rows_chunk)
            x = x_ref[rs, :]
            ms = jnp.mean(x * x, axis=-1, keepdims=True)
            y = x * lax.rsqrt(ms + NORM_EPS) * g_ref[...]
            sc = sc_ref[rs, :] if per_row_mod else sc_ref[...]
            sh = sh_ref[rs, :] if per_row_mod else sh_ref[...]
            hb = (y * (1.0 + sc) + sh).astype(BF16)
            h_scr[rs, :] = hb
            oba_ref[rs, :] = _dot(hb, wba)
            return carry
        lax.fori_loop(0, x_ref.shape[0] // norm_rows_chunk, norm_rows, 0)

    acc = _dot(h_scr[...], w_ref[...].astype(BF16))

    @pl.when(j < n_rope_tiles)
    def _():
        scale = jnp.where(j < n_q_tiles, q_scale, 1.0).astype(F32)
        cos = cos_ref[...] * scale
        sin = sin_ref[...] * scale
        for h in range(heads_per_tile):
            xh = acc[:, h * HEAD_DIM:(h + 1) * HEAD_DIM]
            o_ref[:, h * HEAD_DIM:(h + 1) * HEAD_DIM] = xh * cos + pltpu.roll(xh, HEAD_DIM // 2, 1) * sin

    @pl.when(j >= n_rope_tiles)
    def _():
        o_ref[...] = acc


def in_proj(x, scale, shift, g, cos, sin, w_in, w_ba, *, tm, d_attn):
    rows, d = x.shape
    tm = min(tm, rows)
    tn = 512
    n_main = (w_in.shape[1] // tn) * tn
    mod_rows = scale.shape[0]
    mod_block = (tm, d) if mod_rows == rows else (1, d)
    mod_map = (lambda i, j: (i, 0)) if mod_rows == rows else (lambda i, j: (0, 0))
    kern = functools.partial(_inproj_kernel, n_rope_tiles=2 * d_attn // tn, n_q_tiles=d_attn // tn,
                             heads_per_tile=tn // HEAD_DIM, q_scale=HEAD_DIM ** -0.5, norm_rows_chunk=min(256, tm))
    return pl.pallas_call(
        kern,
        grid=(rows // tm, n_main // tn),
        in_specs=[pl.BlockSpec((tm, d), lambda i, j: (i, 0)),
                  pl.BlockSpec(mod_block, mod_map),
                  pl.BlockSpec(mod_block, mod_map),
                  pl.BlockSpec((1, d), lambda i, j: (0, 0)),
                  pl.BlockSpec((tm, HEAD_DIM), lambda i, j: (i, 0)),
                  pl.BlockSpec((tm, HEAD_DIM), lambda i, j: (i, 0)),
                  pl.BlockSpec((d, tn), lambda i, j: (0, j)),
                  pl.BlockSpec((d, LANES), lambda i, j: (0, 0))],
        out_specs=[pl.BlockSpec((tm, tn), lambda i, j: (i, j)),
                   pl.BlockSpec((tm, LANES), lambda i, j: (i, 0))],
        out_shape=[jax.ShapeDtypeStruct((rows, n_main), F32),
                   jax.ShapeDtypeStruct((rows, LANES), F32)],
        scratch_shapes=[pltpu.VMEM((tm, d), BF16)],
        compiler_params=_cparams("arbitrary", "arbitrary"),
        name="in_proj",
    )(x, scale, shift, g.reshape(1, d), cos, sin, w_in, w_ba)


ATTN_SPAN = ATTN_BLOCK * max(d for _, d in DILATED_PATTERNS)
ATTN_UNROLL = 16


def _strided_rows(start, size, stride):
    return pl.ds(start, size) if stride == 1 else pl.ds(start, size, stride=stride)


def _attn_prompt_kernel(q_ref, kp_ref, kc_ref, vp_ref, vc_ref, o_ref, kbuf, vbuf, *stat_bufs):
    n = pl.program_id(0)
    span = ATTN_SPAN
    blk = ATTN_BLOCK
    kbuf[0:span, :] = kp_ref[...]
    kbuf[span:2 * span, :] = kc_ref[...]
    vbuf[0:span, :] = vp_ref[...]
    vbuf[span:2 * span, :] = vc_ref[...]
    qi = lax.broadcasted_iota(jnp.int32, (blk, 2 * blk), 0)
    kj = lax.broadcasted_iota(jnp.int32, (blk, 2 * blk), 1)
    dist = qi + blk - kj
    bias = jnp.where(dist < 0, NEG_BIG, jnp.where(dist > blk, NEG_BIG, 0.0)).astype(F32)
    bias_first = jnp.where(kj < blk, NEG_BIG, bias)

    for pi, (_, dil) in enumerate(DILATED_PATTERNS):
        o_buf, l_buf = stat_bufs[2 * pi], stat_bufs[2 * pi + 1]
        unit_span = blk * dil

        def unit(u, carry, dil=dil, unit_span=unit_span, o_buf=o_buf, l_buf=l_buf):
            sub = u // dil
            start = sub * unit_span + (u - sub * dil)
            q = q_ref[_strided_rows(start, blk, dil), :].astype(BF16)
            kcat = kbuf[_strided_rows(span + start - unit_span, 2 * blk, dil), :].astype(BF16)
            vcat = vbuf[_strided_rows(span + start - unit_span, 2 * blk, dil), :].astype(BF16)
            first = jnp.logical_and(n == 0, sub == 0)
            s = _dot_nt(q, kcat) + jnp.where(first, bias_first, bias)
            m = jnp.max(s, axis=-1, keepdims=True)
            p = jnp.exp(s - m)
            l = jnp.sum(p, axis=-1, keepdims=True)
            o_buf[_strided_rows(start, blk, dil), :] = _dot(p.astype(BF16), vcat) / l
            l_buf[_strided_rows(start, blk, dil), :] = jnp.broadcast_to(m + jnp.log(l), (blk, HEAD_DIM))
            return carry
        lax.fori_loop(0, span // blk, unit, 0, unroll=ATTN_UNROLL)

    def merge(bi, carry):
        rs = pl.ds(pl.multiple_of(bi * blk, blk), blk)
        lses = [stat_bufs[2 * pi + 1][rs, :] for pi in range(len(DILATED_PATTERNS))]
        mx = lses[0]
        for x in lses[1:]:
            mx = jnp.maximum(mx, x)
        num = jnp.zeros((blk, HEAD_DIM), F32)
        den = jnp.zeros((blk, HEAD_DIM), F32)
        for pi, x in enumerate(lses):
            wgt = jnp.exp(x - mx)
            num = num + wgt * stat_bufs[2 * pi][rs, :]
            den = den + wgt
        o_ref[rs, :] = num / den
        return carry
    lax.fori_loop(0, span // blk, merge, 0)


def attn_prompt(p, *, n_heads):
    s = p.shape[0]
    span = ATTN_SPAN
    assert s % span == 0
    for window, dil in DILATED_PATTERNS:
        assert window // dil == ATTN_BLOCK and dil & (dil - 1) == 0
    blk = (span, HEAD_DIM)
    n_stats = 2 * len(DILATED_PATTERNS)
    return pl.pallas_call(
        _attn_prompt_kernel,
        grid=(s // span, n_heads),
        in_specs=[pl.BlockSpec(blk, lambda n, h: (n, h)),
                  pl.BlockSpec(blk, lambda n, h: (jnp.maximum(n - 1, 0), n_heads + h)),
                  pl.BlockSpec(blk, lambda n, h: (n, n_heads + h)),
                  pl.BlockSpec(blk, lambda n, h: (jnp.maximum(n - 1, 0), 2 * n_heads + h)),
                  pl.BlockSpec(blk, lambda n, h: (n, 2 * n_heads + h))],
        out_specs=pl.BlockSpec(blk, lambda n, h: (n, h)),
        out_shape=jax.ShapeDtypeStruct((s, n_heads * HEAD_DIM), F32),
        scratch_shapes=[pltpu.VMEM((2 * span, HEAD_DIM), F32), pltpu.VMEM((2 * span, HEAD_DIM), F32)]
                       + [pltpu.VMEM((span, HEAD_DIM), F32)] * n_stats,
        compiler_params=_cparams("arbitrary", "arbitrary"),
        name="attn_prompt",
    )(p, p, p, p, p)


def _attn_sample_kernel(q_ref, kn_ref, vn_ref, kf_ref, vf_ref, kc_ref, vc_ref, cntf_ref, cntc_ref, o_ref,
                        *, n_heads, t_new, half):
    mf = kf_ref.shape[1]
    near = kc_ref.shape[1] // n_heads
    cnt_f = cntf_ref[...]
    cnt_c = cntc_ref[...]
    bias_f = jnp.where(cnt_f > 0.0, 0.0, NEG_BIG)
    bias_c = jnp.where(cnt_c > 0.0, 0.0, NEG_BIG)
    ti = lax.broadcasted_iota(jnp.int32, (t_new, 1), 0)
    for h in range(n_heads):
        sl = slice(h * HEAD_DIM, (h + 1) * HEAD_DIM)
        q = q_ref[:, sl]
        kn = kn_ref[:, sl]
        vn = vn_ref[:, sl]
        q16 = q.astype(BF16)
        far_rows = pl.ds(h, half, stride=n_heads)
        k_far = kf_ref[0, :, 0, far_rows, :].reshape(mf * half, HEAD_DIM).astype(BF16)
        v_far = vf_ref[0, :, 0, far_rows, :].reshape(mf * half, HEAD_DIM).astype(BF16)
        near_rows = pl.ds(h, near, stride=n_heads)
        k_near = kc_ref[0, near_rows, :].astype(BF16)
        v_near = vc_ref[0, near_rows, :].astype(BF16)
        s_f = _dot_nt(q16, k_far) + bias_f
        s_c = _dot_nt(q16, k_near) + bias_c
        m = jnp.maximum(jnp.max(s_f, axis=-1, keepdims=True), jnp.max(s_c, axis=-1, keepdims=True))
        s_n, c_n = [], []
        for t2 in range(t_new):
            diff = ti - t2
            c = jnp.zeros((t_new, 1), F32)
            for window, dil in DILATED_PATTERNS:
                ok = (diff >= 0) & ((diff & (dil - 1)) == 0) & (diff <= window)
                c = c + jnp.where(ok, 1.0, 0.0)
            sn = jnp.where(c > 0.0, jnp.sum(q * kn[t2:t2 + 1, :], axis=-1, keepdims=True), NEG_BIG)
            m = jnp.maximum(m, sn)
            s_n.append(sn)
            c_n.append(c)
        e_f = cnt_f * jnp.exp(s_f - m)
        e_c = cnt_c * jnp.exp(s_c - m)
        l = jnp.sum(e_f, axis=-1, keepdims=True) + jnp.sum(e_c, axis=-1, keepdims=True)
        acc = _dot(e_f.astype(BF16), v_far) + _dot(e_c.astype(BF16), v_near)
        for t2 in range(t_new):
            e_n = c_n[t2] * jnp.exp(s_n[t2] - m)
            l = l + e_n
            acc = acc + e_n * vn[t2:t2 + 1, :]
        o_ref[:, sl] = acc / l


def _sample_key_counts(t_new, cache_len):
    t = np.arange(t_new)[:, None]
    p = np.arange(cache_len)[None, :]
    diff = cache_len + t - p
    cnt = np.zeros((t_new, cache_len), np.float32)
    for window, dil in DILATED_PATTERNS:
        cnt += ((diff % dil == 0) & (diff <= window)).astype(np.float32)
    return cnt


def attn_sample(p, cache_k, cache_v, *, t_new):
    bt = p.shape[0]
    b = bt // t_new
    _, cache_len, n_heads, hd = cache_k.shape
    d_attn = n_heads * hd
    d_max = max(d for _, d in DILATED_PATTERNS)
    half = d_max // 2
    near = max(w for w, d in DILATED_PATTERNS if d != d_max)
    assert hd == HEAD_DIM and cache_len >= max(w for w, _ in DILATED_PATTERNS)
    assert t_new <= half and cache_len % d_max == 0 and near % d_max == 0 and cache_len % near == 0
    for _, dil in DILATED_PATTERNS:
        assert dil & (dil - 1) == 0
    mf = (cache_len - near) // d_max
    cnt = _sample_key_counts(t_new, cache_len)
    far_pos = (np.arange(mf)[:, None] * d_max + np.arange(half)[None, :]).reshape(-1)
    covered = np.zeros(cache_len, bool)
    covered[far_pos] = True
    covered[cache_len - near:] = True
    assert not cnt[:, ~covered].any()
    cnt_far = jnp.asarray(cnt[:, far_pos])
    cnt_near = jnp.asarray(cnt[:, cache_len - near:])
    far_shape = (b, cache_len // d_max, 2, half * n_heads, hd)
    near_shape = (b, cache_len * n_heads, hd)
    kern = functools.partial(_attn_sample_kernel, n_heads=n_heads, t_new=t_new, half=half)
    blk = (t_new, d_attn)
    far_blk = pl.BlockSpec((1, mf, 1, half * n_heads, hd), lambda i: (i, 0, 0, 0, 0))
    near_blk = pl.BlockSpec((1, near * n_heads, hd), lambda i: (i, cache_len // near - 1, 0))
    return pl.pallas_call(
        kern,
        grid=(b,),
        in_specs=[pl.BlockSpec(blk, lambda i: (i, 0)),
                  pl.BlockSpec(blk, lambda i: (i, 1)),
                  pl.BlockSpec(blk, lambda i: (i, 2)),
                  far_blk, far_blk, near_blk, near_blk,
                  pl.BlockSpec(cnt_far.shape, lambda i: (0, 0)),
                  pl.BlockSpec(cnt_near.shape, lambda i: (0, 0))],
        out_specs=pl.BlockSpec(blk, lambda i: (i, 0)),
        out_shape=jax.ShapeDtypeStruct((bt, d_attn), F32),
        compiler_params=_cparams("arbitrary"),
        name="attn_sample",
    )(p, p, p, cache_k.reshape(far_shape), cache_v.reshape(far_shape),
      cache_k.reshape(near_shape), cache_v.reshape(near_shape), cnt_far, cnt_near)


def _causal_conv_silu(x, prev8, cw):
    t = x.shape[0]
    row8 = lax.broadcasted_iota(jnp.int32, (SUBLANES, 1), 0)
    conv = x * cw[CONV_W - 1:CONV_W, :]
    for i in range(1, CONV_W):
        xr = pltpu.roll(x, i, 0)
        pr = pltpu.roll(prev8, i, 0)
        head = jnp.where(row8 < i, pr, xr[:SUBLANES, :])
        xs = head if t == SUBLANES else jnp.concatenate([head, xr[SUBLANES:, :]], axis=0)
        conv = conv + xs * cw[CONV_W - 1 - i:CONV_W - i, :]
    return conv * _sigmoid(conv)


def _l2norm(x):
    return x * lax.rsqrt(jnp.sum(x * x, axis=-1, keepdims=True) + L2_EPS)


def _gated_rmsnorm(o, g_row, z):
    y = o * lax.rsqrt(jnp.mean(o * o, axis=-1, keepdims=True) + NORM_EPS) * g_row
    return y * (z * _sigmoid(z))


def _beta_and_g(ba, alog_row, dtb_row):
    beta = _sigmoid(ba)
    g = -jnp.exp(alog_row) * _softplus(ba + dtb_row)
    return beta, g


def _gdn_prompt_kernel(qkv_ref, z_ref, ba_ref, cw_ref, alog_ref, dtb_ref, gn_ref,
                       o_ref, tail_ref, ssm_ref, s_scr, prev_scr, *, n_heads):
    n = pl.program_id(0)
    c = GDN_CHUNK
    hd = HEAD_DIM

    @pl.when(n == 0)
    def _():
        s_scr[...] = jnp.zeros_like(s_scr)
        prev_scr[...] = jnp.zeros_like(prev_scr)

    x = qkv_ref[...]
    conv = _causal_conv_silu(x, prev_scr[...], cw_ref[...])
    prev_scr[...] = x[c - SUBLANES:, :]
    tail_ref[...] = x[c - SUBLANES:, :]

    beta_all, g_all = _beta_and_g(ba_ref[...], alog_ref[...], dtb_ref[...])
    ri = lax.broadcasted_iota(jnp.int32, (c, c), 0)
    ci = lax.broadcasted_iota(jnp.int32, (c, c), 1)
    tri_incl = ri >= ci
    tri_strict = ri > ci
    eye = jnp.where(ri == ci, 1.0, 0.0).astype(F32)
    blk_xor = ri ^ ci
    tri_b = jnp.where(tri_incl, 1.0, 0.0).astype(BF16)
    g_hi, g_mid, g_lo = _split3(g_all)
    gcum = _dot(tri_b, g_hi) + _dot(tri_b, g_mid) + _dot(tri_b, g_lo)
    gcum_t = gcum.T
    gn = gn_ref[...]

    hs = range(n_heads)
    q = [_l2norm(conv[:, h * hd:(h + 1) * hd]) * (hd ** -0.5) for h in hs]
    k = [_l2norm(conv[:, (n_heads + h) * hd:(n_heads + h + 1) * hd]) for h in hs]
    v = [conv[:, (2 * n_heads + h) * hd:(2 * n_heads + h + 1) * hd] for h in hs]
    beta = [beta_all[:, h:h + 1] for h in hs]
    gc = [gcum[:, n_heads + h:n_heads + h + 1] for h in hs]
    gr = [gcum_t[n_heads + h:n_heads + h + 1, :] for h in hs]
    g_last = [gcum[c - 1:c, n_heads + h:n_heads + h + 1] for h in hs]
    decay = [jnp.where(tri_incl, jnp.exp(gc[h] - gr[h]), 0.0) for h in hs]
    kb = [k[h] * beta[h] for h in hs]
    k16 = [k[h].astype(BF16) for h in hs]
    a = [jnp.where(tri_strict, _dot_nt(kb[h].astype(BF16), k16[h]) * decay[h], 0.0) for h in hs]
    qk = [_dot_nt(q[h].astype(BF16), k16[h]) * decay[h] for h in hs]
    t_inv = [eye - jnp.where((blk_xor >> 1) == 0, a[h], 0.0) for h in hs]
    for lvl in range(2, int(math.log2(c)) + 1):
        join = (blk_xor >> (lvl - 1)) == 1
        t16 = [t_inv[h].astype(BF16) for h in hs]
        lt = [_dot(jnp.where(join, a[h], 0.0).astype(BF16), t16[h]).astype(BF16) for h in hs]
        t_inv = [t_inv[h] - _dot(t16[h], lt[h]) for h in hs]
    rhs = [jnp.concatenate([v[h] * beta[h], kb[h] * jnp.exp(gc[h])], axis=-1).astype(BF16) for h in hs]
    sol = [_dot(t_inv[h].astype(BF16), rhs[h]) for h in hs]
    s_old = [s_scr[h] for h in hs]
    s16 = [s_old[h].astype(BF16) for h in hs]
    v_new = [sol[h][:, :hd] - _dot(sol[h][:, hd:].astype(BF16), s16[h]) for h in hs]
    v16 = [v_new[h].astype(BF16) for h in hs]
    o = [_dot((q[h] * jnp.exp(gc[h])).astype(BF16), s16[h]) + _dot(qk[h].astype(BF16), v16[h]) for h in hs]
    k_dec = [(k[h] * jnp.exp(g_last[h] - gc[h])).T.astype(BF16) for h in hs]
    for h in hs:
        s_scr[h] = s_old[h] * jnp.exp(g_last[h]) + _dot(k_dec[h], v16[h])
        o_ref[:, h * hd:(h + 1) * hd] = _gated_rmsnorm(o[h], gn, z_ref[:, h * hd:(h + 1) * hd])

    @pl.when(n == pl.num_programs(0) - 1)
    def _():
        ssm_ref[...] = s_scr[...]


def gdn_prompt(p, pba, conv_w, alog_row, dtb_row, g_norm, *, n_heads, qkv_col_block, z_col_block):
    s = p.shape[0]
    c = GDN_CHUNK
    hd = HEAD_DIM
    conv_dim = 3 * n_heads * hd
    kern = functools.partial(_gdn_prompt_kernel, n_heads=n_heads)
    return pl.pallas_call(
        kern,
        grid=(s // c,),
        in_specs=[pl.BlockSpec((c, conv_dim), lambda n: (n, qkv_col_block)),
                  pl.BlockSpec((c, n_heads * hd), lambda n: (n, z_col_block)),
                  pl.BlockSpec((c, LANES), lambda n: (n, 0)),
                  pl.BlockSpec((CONV_W, conv_dim), lambda n: (0, 0)),
                  pl.BlockSpec((1, LANES), lambda n: (0, 0)),
                  pl.BlockSpec((1, LANES), lambda n: (0, 0)),
                  pl.BlockSpec((1, hd), lambda n: (0, 0))],
        out_specs=[pl.BlockSpec((c, n_heads * hd), lambda n: (n, 0)),
                   pl.BlockSpec((SUBLANES, conv_dim), lambda n: (0, 0)),
                   pl.BlockSpec((n_heads, hd, hd), lambda n: (0, 0, 0))],
        out_shape=[jax.ShapeDtypeStruct((s, n_heads * hd), F32),
                   jax.ShapeDtypeStruct((SUBLANES, conv_dim), F32),
                   jax.ShapeDtypeStruct((n_heads, hd, hd), F32)],
        scratch_shapes=[pltpu.VMEM((n_heads, hd, hd), F32), pltpu.VMEM((SUBLANES, conv_dim), F32)],
        compiler_params=_cparams("arbitrary"),
        name="gdn_prompt",
    )(p, p, pba, conv_w, alog_row, dtb_row, g_norm.reshape(1, hd))


def _gdn_sample_kernel(qkv_ref, z_ref, ba_ref, prev_ref, s0_ref, cw_ref, alog_ref, dtb_ref, gn_ref,
                       o_ref, ssm_ref, *, n_heads, t_new, n_seq):
    hd = HEAD_DIM
    cw = cw_ref[...]
    gn = gn_ref[...]
    beta_all, g_all = _beta_and_g(ba_ref[...], alog_ref[...], dtb_ref[...])
    eg_all = jnp.exp(g_all)
    pad = jnp.zeros((hd - 2 * t_new, hd), F32)
    chains = [(bi, h) for bi in range(n_seq) for h in range(n_heads)]
    conv = [_causal_conv_silu(qkv_ref[bi * t_new:(bi + 1) * t_new, :], prev_ref[bi], cw) for bi in range(n_seq)]
    q = {(bi, h): _l2norm(conv[bi][:, h * hd:(h + 1) * hd]) * (hd ** -0.5) for bi, h in chains}
    k = {(bi, h): _l2norm(conv[bi][:, (n_heads + h) * hd:(n_heads + h + 1) * hd]) for bi, h in chains}
    v = {(bi, h): conv[bi][:, (2 * n_heads + h) * hd:(2 * n_heads + h + 1) * hd] for bi, h in chains}
    kq_t = {c: jnp.concatenate([k[c], q[c], pad], axis=0).T for c in chains}
    s = {(bi, h): s0_ref[bi, h] for bi, h in chains}
    rows = {c: [] for c in chains}
    for t in range(t_new):
        for bi, h in chains:
            c = (bi, h)
            r = bi * t_new + t
            kb = jnp.broadcast_to(kq_t[c][:, t:t + 1], (hd, hd))
            qb = jnp.broadcast_to(kq_t[c][:, t_new + t:t_new + t + 1], (hd, hd))
            eg = eg_all[r:r + 1, n_heads + h:n_heads + h + 1]
            bt = beta_all[r:r + 1, h:h + 1]
            sd = s[c] * eg
            ks = jnp.sum(sd * kb, axis=0, keepdims=True)
            v_new = bt * (v[c][t:t + 1, :] - ks)
            s[c] = sd + kb * v_new
            rows[c].append(jnp.sum(s[c] * qb, axis=0, keepdims=True))
    for bi, h in chains:
        ssm_ref[bi, h] = s[(bi, h)]
        o = jnp.concatenate(rows[(bi, h)], axis=0)
        rs = slice(bi * t_new, (bi + 1) * t_new)
        o_ref[rs, h * hd:(h + 1) * hd] = _gated_rmsnorm(o, gn, z_ref[rs, h * hd:(h + 1) * hd])


def gdn_sample(p, pba, prev8, s0, conv_w, alog_row, dtb_row, g_norm, *, n_heads, t_new, qkv_col_block, z_col_block):
    bt = p.shape[0]
    b = bt // t_new
    hd = HEAD_DIM
    conv_dim = 3 * n_heads * hd
    n_seq = math.gcd(GDN_SAMPLE_SEQS, b)
    rows = n_seq * t_new
    kern = functools.partial(_gdn_sample_kernel, n_heads=n_heads, t_new=t_new, n_seq=n_seq)
    return pl.pallas_call(
        kern,
        grid=(b // n_seq,),
        in_specs=[pl.BlockSpec((rows, conv_dim), lambda i: (i, qkv_col_block)),
                  pl.BlockSpec((rows, n_heads * hd), lambda i: (i, z_col_block)),
                  pl.BlockSpec((rows, LANES), lambda i: (i, 0)),
                  pl.BlockSpec((n_seq, SUBLANES, conv_dim), lambda i: (i, 0, 0)),
                  pl.BlockSpec((n_seq, n_heads, hd, hd), lambda i: (i, 0, 0, 0)),
                  pl.BlockSpec((CONV_W, conv_dim), lambda i: (0, 0)),
                  pl.BlockSpec((1, LANES), lambda i: (0, 0)),
                  pl.BlockSpec((1, LANES), lambda i: (0, 0)),
                  pl.BlockSpec((1, hd), lambda i: (0, 0))],
        out_specs=[pl.BlockSpec((rows, n_heads * hd), lambda i: (i, 0)),
                   pl.BlockSpec((n_seq, n_heads, hd, hd), lambda i: (i, 0, 0, 0))],
        out_shape=[jax.ShapeDtypeStruct((bt, n_heads * hd), F32),
                   jax.ShapeDtypeStruct((b, n_heads, hd, hd), F32)],
        compiler_params=_cparams("arbitrary"),
        name="gdn_sample",
    )(p, p, pba, prev8, s0, conv_w, alog_row, dtb_row, g_norm.reshape(1, hd))


def _outproj_kernel(oa_ref, ob_ref, x_ref, gate_ref, sc_ref, sh_ref, gpost_ref, gpre_ref, wo_ref, wr_hi_ref, wr_lo_ref,
                    br_ref, x1_ref, h2_ref, tw_ref, ti_ref):
    d_attn = oa_ref.shape[1]
    mix = (_dot(oa_ref[...].astype(BF16), wo_ref[:d_attn, :]) + _dot(ob_ref[...].astype(BF16), wo_ref[d_attn:, :]))
    y = mix * lax.rsqrt(jnp.mean(mix * mix, axis=-1, keepdims=True) + NORM_EPS) * gpost_ref[...]
    x1 = x_ref[...] + gate_ref[...] * y
    x1_ref[...] = x1
    hn = x1 * lax.rsqrt(jnp.mean(x1 * x1, axis=-1, keepdims=True) + NORM_EPS) * gpre_ref[...]
    h2 = hn * (1.0 + sc_ref[...]) + sh_ref[...]
    h2_ref[...] = _pack_bf16_pairs(h2)

    h_hi = h2.astype(BF16)
    h_lo = (h2 - h_hi.astype(F32)).astype(BF16)
    logits = (_dot(h_hi, wr_hi_ref[...]) + _dot(h_hi, wr_lo_ref[...]) + _dot(h_lo, wr_hi_ref[...])) + br_ref[...]
    lane = lax.broadcasted_iota(jnp.int32, logits.shape, 1)
    tops, idxs = [], []
    for _ in range(TOP_K):
        mx = jnp.max(logits, axis=-1, keepdims=True)
        ix = jnp.min(jnp.where(logits == mx, lane, LANES), axis=-1, keepdims=True)
        tops.append(mx)
        idxs.append(ix)
        logits = jnp.where(lane == ix, 2.0 * NEG_BIG, logits)
    es = [jnp.exp(t - tops[0]) for t in tops]
    tot = es[0]
    for e in es[1:]:
        tot = tot + e
    tw = jnp.zeros(logits.shape, F32)
    ti = jnp.zeros(logits.shape, jnp.int32)
    for kk in range(TOP_K):
        tw = jnp.where(lane == kk, es[kk] / tot, tw)
        ti = jnp.where(lane == kk, idxs[kk], ti)
    tw_ref[...] = tw
    ti_ref[...] = ti


def out_proj(o_a, o_b, x, gate, scale, shift, g_post, g_pre, wo16, wr_hi, wr_lo, br_row, *, tm):
    rows, d = x.shape
    tm = min(tm, rows)
    d_attn = o_a.shape[1]
    mod_rows = gate.shape[0]
    mod_block = (tm, d) if mod_rows == rows else (1, d)
    mod_map = (lambda i: (i, 0)) if mod_rows == rows else (lambda i: (0, 0))
    half = pl.BlockSpec((tm, d_attn), lambda i: (i, 0))
    full = pl.BlockSpec((tm, d), lambda i: (i, 0))
    row_d = pl.BlockSpec((1, d), lambda i: (0, 0))
    lane_blk = pl.BlockSpec((tm, LANES), lambda i: (i, 0))
    in_specs = [half, pl.BlockSpec((tm, d - d_attn), lambda i: (i, 0)), full,
                pl.BlockSpec(mod_block, mod_map), pl.BlockSpec(mod_block, mod_map), pl.BlockSpec(mod_block, mod_map),
                row_d, row_d,
                pl.BlockSpec((d, d), lambda i: (0, 0)),
                pl.BlockSpec((d, LANES), lambda i: (0, 0)), pl.BlockSpec((d, LANES), lambda i: (0, 0)),
                pl.BlockSpec((1, LANES), lambda i: (0, 0))]
    args = [o_a, o_b, x, gate, scale, shift, g_post.reshape(1, d), g_pre.reshape(1, d), wo16, wr_hi, wr_lo, br_row]
    return pl.pallas_call(
        _outproj_kernel,
        grid=(rows // tm,),
        in_specs=in_specs,
        out_specs=[full, pl.BlockSpec((tm, d // 2), lambda i: (i, 0)), lane_blk, lane_blk],
        out_shape=[jax.ShapeDtypeStruct((rows, d), F32), jax.ShapeDtypeStruct((rows, d // 2), jnp.uint32),
                   jax.ShapeDtypeStruct((rows, LANES), F32), jax.ShapeDtypeStruct((rows, LANES), jnp.int32)],
        compiler_params=_cparams("arbitrary"),
        name="out_proj",
    )(*args)


def _pack_bf16_pairs(x):
    half = x.shape[1] // 2
    bits = pltpu.bitcast(x.astype(BF16).astype(F32), jnp.uint32)
    return bits[:, :half] | (bits[:, half:] >> 16)


def _unpack_bf16_pairs(w):
    hi = pltpu.bitcast(w & jnp.uint32(0xFFFF0000), F32).astype(BF16)
    lo = pltpu.bitcast(w << 16, F32).astype(BF16)
    return hi, lo


DISPATCH_TOKENS = 256
ROW_DMA_UNROLL = 8


def _dispatch_kernel(pos_ref, zs_ref, h_ref, x_hbm, zbuf, sem_z, sem):
    i = pl.program_id(0)
    tt = h_ref.shape[0]
    sb = zbuf.shape[0]
    n_items = zs_ref.shape[0]
    n_sub = MOE_BLOCK_ROWS // sb

    def zero_copy(w, s):
        row0 = pl.multiple_of((w * n_sub + s) * sb, sb)
        return pltpu.make_async_copy(zbuf, x_hbm.at[pl.ds(row0, sb), :], sem_z)

    @pl.when(i == 0)
    def _():
        zbuf[...] = jnp.zeros(zbuf.shape, zbuf.dtype)

        def start(w, carry):
            lax.fori_loop(zs_ref[w], n_sub, lambda s, c: (zero_copy(w, s).start(), c)[1], 0)
            return carry
        lax.fori_loop(0, n_items, start, 0)

        def wait(w, carry):
            lax.fori_loop(zs_ref[w], n_sub, lambda s, c: (zero_copy(w, s).wait(), c)[1], 0)
            return carry
        lax.fori_loop(0, n_items, wait, 0)

    base = i * (tt * TOP_K)

    def row_copy(t, kk):
        dst = pos_ref[base + t * TOP_K + kk]
        return pltpu.make_async_copy(h_ref.at[pl.ds(t, 1), :], x_hbm.at[pl.ds(dst, 1), :], sem)

    for kk in range(TOP_K):
        def start(t, carry, kk=kk):
            row_copy(t, kk).start()
            return carry
        lax.fori_loop(0, tt, start, 0, unroll=ROW_DMA_UNROLL)
    for kk in range(TOP_K):
        def wait(t, carry, kk=kk):
            row_copy(t, kk).wait()
            return carry
        lax.fori_loop(0, tt, wait, 0, unroll=ROW_DMA_UNROLL)


def moe_dispatch(h2p, pos_flat, zero_from):
    n_tok, half = h2p.shape
    tt = math.gcd(DISPATCH_TOKENS, n_tok)
    assert tt % SUBLANES == 0
    n_items = zero_from.shape[0]
    grid_spec = pltpu.PrefetchScalarGridSpec(
        num_scalar_prefetch=2,
        grid=(n_tok // tt,),
        in_specs=[pl.BlockSpec((tt, half), lambda i, pos, zs: (i, 0))],
        out_specs=pl.BlockSpec(memory_space=pl.ANY),
        scratch_shapes=[pltpu.VMEM((MOE_SUB_ROWS, half), jnp.uint32), pltpu.SemaphoreType.DMA(()),
                        pltpu.SemaphoreType.DMA(())],
    )
    return pl.pallas_call(
        _dispatch_kernel,
        grid_spec=grid_spec,
        out_shape=jax.ShapeDtypeStruct((n_items * MOE_BLOCK_ROWS, half), jnp.uint32),
        compiler_params=_cparams("arbitrary"),
        name="moe_dispatch",
    )(pos_flat, zero_from, h2p)


def _moe_kernel(e_ref, ns_ref, x_ref, wg_ref, wu_ref, bg_ref, bu_ref, wd_ref, bd_ref, o_ref, xb_scr):
    w = pl.program_id(0)
    j = pl.program_id(1)
    ns = ns_ref[w]
    sb = MOE_SUB_ROWS
    n_sub = MOE_BLOCK_ROWS // sb
    half = x_ref.shape[1]

    @pl.when(j == 0)
    def _():
        def zero(s, carry):
            r0 = pl.multiple_of(s * sb, sb)
            o_ref[pl.ds(r0, sb), :] = jnp.zeros((sb, o_ref.shape[1]), F32)
            return carry
        lax.fori_loop(ns, n_sub, zero, 0)

        def unpack(s, carry):
            rs = pl.ds(pl.multiple_of(s * sb, sb), sb)
            hi, lo = _unpack_bf16_pairs(x_ref[rs, :])
            xb_scr[rs, :half] = hi
            xb_scr[rs, half:] = lo
            o_ref[rs, :] = jnp.broadcast_to(bd_ref[0], (sb, o_ref.shape[1]))
            return carry
        lax.fori_loop(0, ns, unpack, 0)

    def compute(n_rows):
        xs = xb_scr[0:n_rows, :]
        gate = jnp.minimum(_dot(xs, wg_ref[0].astype(BF16)) + bg_ref[0], SWIGLU_LIMIT)
        up = jnp.clip(_dot(xs, wu_ref[0].astype(BF16)) + bu_ref[0], -SWIGLU_LIMIT, SWIGLU_LIMIT)
        act = (gate * _sigmoid(SWIGLU_ALPHA * gate) * (up + 1.0)).astype(BF16)
        wd = wd_ref[0].astype(BF16)
        dc = MOE_DOWN_CHUNK
        for c0 in range(0, o_ref.shape[1], dc):
            o_ref[0:n_rows, c0:c0 + dc] += _dot(act, wd[:, c0:c0 + dc])

    for n in range(1, n_sub + 1):
        pl.when(ns == n)(functools.partial(compute, n * sb))


def moe_experts(x_sorted, item_e, item_ns, w_gate_up, b_gate_up, w_down, b_down):
    n_rows, half = x_sorted.shape
    d = 2 * half
    n_e, _, two_ff = w_gate_up.shape
    d_ff = two_ff // 2
    bm = MOE_BLOCK_ROWS
    tf = MOE_FF_TILE
    n_f = d_ff // tf
    n_items = n_rows // bm

    def jj(j, ns, w):
        return jnp.where(ns[w] > 0, j, n_f - 1)

    grid_spec = pltpu.PrefetchScalarGridSpec(
        num_scalar_prefetch=2,
        grid=(n_items, n_f),
        in_specs=[pl.BlockSpec((bm, half), lambda w, j, e, ns: (w, 0)),
                  pl.BlockSpec((1, d, tf), lambda w, j, e, ns: (e[w], 0, jj(j, ns, w))),
                  pl.BlockSpec((1, d, tf), lambda w, j, e, ns: (e[w], 0, n_f + jj(j, ns, w))),
                  pl.BlockSpec((1, 1, tf), lambda w, j, e, ns: (e[w], 0, jj(j, ns, w))),
                  pl.BlockSpec((1, 1, tf), lambda w, j, e, ns: (e[w], 0, n_f + jj(j, ns, w))),
                  pl.BlockSpec((1, tf, d), lambda w, j, e, ns: (e[w], jj(j, ns, w), 0)),
                  pl.BlockSpec((1, 1, d), lambda w, j, e, ns: (e[w], 0, 0))],
        out_specs=pl.BlockSpec((bm, d), lambda w, j, e, ns: (w, 0)),
        scratch_shapes=[pltpu.VMEM((bm, d), BF16)],
    )
    return pl.pallas_call(
        _moe_kernel,
        grid_spec=grid_spec,
        out_shape=jax.ShapeDtypeStruct((n_rows, d), F32),
        compiler_params=_cparams("arbitrary", "arbitrary"),
        name="moe_experts",
    )(item_e, item_ns, x_sorted, w_gate_up, w_gate_up, b_gate_up.reshape(n_e, 1, two_ff),
      b_gate_up.reshape(n_e, 1, two_ff), w_down, b_down.reshape(n_e, 1, d))


def moe_routing(top_e):
    n_tok = top_e.shape[0]
    n_asg = n_tok * TOP_K
    bm = MOE_BLOCK_ROWS
    sb = MOE_SUB_ROWS
    n_items = n_asg // bm + N_EXPERTS
    i32 = jnp.int32
    onehot = jnp.sum((top_e[:, :, None] == jnp.arange(N_EXPERTS, dtype=i32)[None, None, :]).astype(i32), axis=1)
    counts = jnp.sum(onehot, axis=0)
    earlier = jnp.cumsum(onehot, axis=0) - onehot
    rank = jnp.take_along_axis(earlier, top_e, axis=1)
    n_it = (counts + bm - 1) // bm
    per = ((counts + jnp.maximum(n_it, 1) - 1) // jnp.maximum(n_it, 1) + sb - 1) // sb * sb
    per = jnp.maximum(per, sb)
    it_end = jnp.cumsum(n_it)
    it_base = it_end - n_it
    per_t = per[top_e]
    k_idx = jnp.floor((rank.astype(F32) + 0.5) / per_t.astype(F32)).astype(i32)
    pos = ((it_base[top_e] + k_idx) * bm + rank - k_idx * per_t).astype(i32)
    w = jnp.arange(n_items, dtype=i32)
    total = it_end[-1]
    e_raw = jnp.minimum(jnp.sum((it_end[None, :] <= w[:, None]).astype(i32), axis=1), N_EXPERTS - 1)
    rows = jnp.clip(counts[e_raw] - (w - it_base[e_raw]) * per[e_raw], 0, per[e_raw])
    rows = jnp.where(w < total, rows, 0)
    item_ns = (rows + sb - 1) // sb
    item_e = jnp.where(w < total, e_raw, e_raw[jnp.maximum(total - 1, 0)])
    zero_from = jnp.maximum(item_ns - 1, 0)
    return pos, item_e.astype(i32), item_ns.astype(i32), zero_from.astype(i32)


def _finish_kernel(pos_ref, x1_ref, tw_ref, gate_ref, g_ref, y_hbm, o_ref, ybuf, sem):
    i = pl.program_id(0)
    n_tiles = pl.num_programs(0)
    tm, d = x1_ref.shape

    def row_copy(base, slot, kk, t):
        src = pos_ref[base + t * TOP_K + kk]
        return pltpu.make_async_copy(y_hbm.at[pl.ds(src, 1), :], ybuf.at[slot, kk, pl.ds(t, 1), :], sem.at[slot])

    def issue(tile, slot):
        base = tile * (tm * TOP_K)
        for kk in range(TOP_K):
            def body(t, carry, kk=kk):
                row_copy(base, slot, kk, t).start()
                return carry
            lax.fori_loop(0, tm, body, 0, unroll=ROW_DMA_UNROLL)

    def wait(tile, slot):
        base = tile * (tm * TOP_K)
        for kk in range(TOP_K):
            def body(t, carry, kk=kk):
                row_copy(base, slot, kk, t).wait()
                return carry
            lax.fori_loop(0, tm, body, 0, unroll=ROW_DMA_UNROLL)

    @pl.when(i == 0)
    def _():
        issue(0, 0)

    @pl.when(i + 1 < n_tiles)
    def _():
        issue(i + 1, (i + 1) % 2)

    slot = i % 2
    wait(i, slot)
    tw = tw_ref[...]
    ff = tw[:, 0:1] * ybuf[slot, 0]
    for kk in range(1, TOP_K):
        ff = ff + tw[:, kk:kk + 1] * ybuf[slot, kk]
    y = ff * lax.rsqrt(jnp.mean(ff * ff, axis=-1, keepdims=True) + NORM_EPS) * g_ref[...]
    o_ref[...] = x1_ref[...] + gate_ref[...] * y


def finish(x1, y_sorted, pos, tw, gate, g_post, *, tm):
    rows, d = x1.shape
    tm = min(tm, rows)
    mod_rows = gate.shape[0]
    mod_block = (tm, d) if mod_rows == rows else (1, d)
    mod_map = (lambda i, pos: (i, 0)) if mod_rows == rows else (lambda i, pos: (0, 0))
    grid_spec = pltpu.PrefetchScalarGridSpec(
        num_scalar_prefetch=1,
        grid=(rows // tm,),
        in_specs=[pl.BlockSpec((tm, d), lambda i, pos: (i, 0)),
                  pl.BlockSpec((tm, LANES), lambda i, pos: (i, 0)),
                  pl.BlockSpec(mod_block, mod_map),
                  pl.BlockSpec((1, d), lambda i, pos: (0, 0)),
                  pl.BlockSpec(memory_space=pl.ANY)],
        out_specs=pl.BlockSpec((tm, d), lambda i, pos: (i, 0)),
        scratch_shapes=[pltpu.VMEM((2, TOP_K, tm, d), F32), pltpu.SemaphoreType.DMA((2,))],
    )
    return pl.pallas_call(
        _finish_kernel,
        grid_spec=grid_spec,
        out_shape=jax.ShapeDtypeStruct((rows, d), F32),
        compiler_params=_cparams("arbitrary"),
        name="finish",
    )(pos, x1, tw, gate, g_post.reshape(1, d), y_sorted)


def _rope_tables(pos):
    half = HEAD_DIM // 2
    inv_freq = ROPE_THETA ** (-np.arange(half, dtype=np.float64) / half)
    ang = np.asarray(pos, np.float64)[:, None] * inv_freq[None, :]
    cos = np.cos(ang)
    sin = np.sin(ang)
    return (jnp.asarray(np.concatenate([cos, cos], axis=-1), F32),
            jnp.asarray(np.concatenate([-sin, sin], axis=-1), F32))


def _lane_row(vals, offset):
    return jnp.zeros((1, LANES), F32).at[0, offset:offset + vals.shape[0]].set(vals.astype(F32))


def kernel(x_prompt, x_sample, c_prompt, c_sample, cache_win_k, cache_win_v, state_conv, state_ssm, w_ada, b_ada, g_pre_mix, g_post_mix, g_pre_ffn, g_post_ffn, w_in, conv_w, a_log, dt_bias, g_gdn_norm, w_out, w_router, b_router, w_gate_up, b_gate_up, w_down, b_down):
    depth = w_ada.shape[0]
    assert depth == 1 and x_prompt.shape[0] == 1
    _, s_len, d = x_prompt.shape
    b_s, t_s, _ = x_sample.shape
    n_heads_a = cache_win_k.shape[3]
    d_attn = n_heads_a * HEAD_DIM
    n_heads_b = state_ssm.shape[2]
    conv_dim = 3 * n_heads_b * HEAD_DIM
    n_samp = b_s * t_s
    win_p = min(max(w for w, _ in DILATED_PATTERNS), s_len)

    c_all = jnp.concatenate([c_prompt, c_sample], axis=0)
    c_rows = -(-c_all.shape[0] // SUBLANES) * SUBLANES
    c_all = jnp.pad(c_all, ((0, c_rows - c_all.shape[0]), (0, 0)))
    mod = ada_mod(c_all, w_ada[0], b_ada[0])
    mod_p = [mod[0:1, i * d:(i + 1) * d] for i in range(6)]
    mod_s = [jnp.repeat(mod[1:1 + b_s, i * d:(i + 1) * d], t_s, axis=0) for i in range(6)]

    n_main = d_attn * 3 + conv_dim + n_heads_b * HEAD_DIM
    w_ba = jnp.pad(w_in[0][:, n_main:], ((0, 0), (0, LANES - (w_in.shape[2] - n_main))))
    cos_p, sin_p = _rope_tables(np.arange(s_len))
    cos_s, sin_s = _rope_tables(PAST_LEN + (np.arange(n_samp) % t_s))
    xp = x_prompt[0]
    xs = x_sample.reshape(n_samp, d)
    p_p, pba_p = in_proj(xp, mod_p[1], mod_p[0], g_pre_mix[0], cos_p, sin_p, w_in[0], w_ba, tm=1024, d_attn=d_attn)
    p_s, pba_s = in_proj(xs, mod_s[1], mod_s[0], g_pre_mix[0], cos_s, sin_s, w_in[0], w_ba, tm=512, d_attn=d_attn)

    oa_p = attn_prompt(p_p, n_heads=n_heads_a)
    oa_s = attn_sample(p_s, cache_win_k[0], cache_win_v[0], t_new=t_s)

    alog_row = _lane_row(a_log[0], n_heads_b)
    dtb_row = _lane_row(dt_bias[0], n_heads_b)
    qkv_blk = (3 * d_attn) // conv_dim
    z_blk = (3 * d_attn + conv_dim) // (n_heads_b * HEAD_DIM)
    assert qkv_blk * conv_dim == 3 * d_attn and z_blk * n_heads_b * HEAD_DIM == 3 * d_attn + conv_dim
    ob_p, tail_p, ssm_p = gdn_prompt(p_p, pba_p, conv_w[0], alog_row, dtb_row, g_gdn_norm[0],
                                     n_heads=n_heads_b, qkv_col_block=qkv_blk, z_col_block=z_blk)
    prev8 = jnp.pad(state_conv[0], ((0, 0), (SUBLANES - (CONV_W - 1), 0), (0, 0)))
    ob_s, ssm_s = gdn_sample(p_s, pba_s, prev8, state_ssm[0], conv_w[0], alog_row, dtb_row, g_gdn_norm[0],
                             n_heads=n_heads_b, t_new=t_s, qkv_col_block=qkv_blk, z_col_block=z_blk)

    wo16 = w_out[0].astype(BF16)
    wr = jnp.pad(w_router[0], ((0, 0), (0, LANES - N_EXPERTS)))
    wr_hi = wr.astype(BF16)
    wr_lo = (wr - wr_hi.astype(F32)).astype(BF16)
    br_row = jnp.full((1, LANES), NEG_BIG, F32).at[0, :N_EXPERTS].set(b_router[0])
    x1_p, h2_p, tw_p, ti_p = out_proj(oa_p, ob_p, xp, mod_p[2], mod_p[4], mod_p[3], g_post_mix[0], g_pre_ffn[0],
                                      wo16, wr_hi, wr_lo, br_row, tm=256)
    x1_s, h2_s, tw_s, ti_s = out_proj(oa_s, ob_s, xs, mod_s[2], mod_s[4], mod_s[3], g_post_mix[0], g_pre_ffn[0],
                                      wo16, wr_hi, wr_lo, br_row, tm=256)

    h2 = jnp.concatenate([h2_p, h2_s], axis=0)
    top_e = jnp.concatenate([ti_p[:, :TOP_K], ti_s[:, :TOP_K]], axis=0)
    pos, item_e, item_ns, zero_from = moe_routing(top_e)
    x_sorted = moe_dispatch(h2, pos.reshape(-1), zero_from)
    y_sorted = moe_experts(x_sorted, item_e, item_ns, w_gate_up[0], b_gate_up[0], w_down[0], b_down[0])
    y_p = finish(x1_p, y_sorted, pos[:s_len].reshape(-1), tw_p, mod_p[5], g_post_ffn[0], tm=128)
    y_s = finish(x1_s, y_sorted, pos[s_len:].reshape(-1), tw_s, mod_s[5], g_post_ffn[0], tm=128)

    hd = HEAD_DIM
    k_p = p_p[s_len - win_p:, d_attn:2 * d_attn].reshape(1, 1, win_p, n_heads_a, hd)
    v_p = p_p[s_len - win_p:, 2 * d_attn:3 * d_attn].reshape(1, 1, win_p, n_heads_a, hd)
    conv_p = tail_p[SUBLANES - (CONV_W - 1):].reshape(1, 1, CONV_W - 1, conv_dim)
    k_s = p_s[:, d_attn:2 * d_attn].reshape(1, b_s, t_s, n_heads_a, hd)
    v_s = p_s[:, 2 * d_attn:3 * d_attn].reshape(1, b_s, t_s, n_heads_a, hd)
    conv_s = p_s.reshape(b_s, t_s, -1)[:, t_s - (CONV_W - 1):, 3 * d_attn:3 * d_attn + conv_dim][None]
    return (y_p[None], y_s.reshape(b_s, t_s, d), k_p, v_p, conv_p, ssm_p[None, None],
            k_s, v_s, conv_s, ssm_s[None])
```
